```python
import math
import jax, jax.numpy as jnp
from jax import lax
import numpy as np

D_MODEL = 2048
BATCH = 1
SEQ = 16384
DEPTH = 4
DEC_BATCH = 16
DEC_SEQ = 32
PAST_LEN = 1024

CHUNK = 64
Q_BLOCK = 128
KEY_BLOCK = 128
HEAD_DIM = 128
H_SB = 4
H_FOX = 4
H_DIFF = 4
DIFF_QK = HEAD_DIM // 2
W_SB = H_SB * HEAD_DIM
W_FOX = H_FOX * HEAD_DIM
W_DIFF = H_DIFF * HEAD_DIM
D_MIX = W_SB + W_FOX + W_DIFF
D_FF = 5632
D_FF_E = 1408
N_EXPERTS = 8
TOP_K = 2
N_DENSE = (DEPTH + 1) // 2
N_MOE = DEPTH // 2
RMS_EPS = 1e-6
IN_SIZES = (W_SB, W_SB, W_SB, W_FOX, W_FOX, W_FOX, H_FOX, W_DIFF, W_DIFF, W_DIFF)
N_IN = 3 * W_SB + 3 * W_FOX + H_FOX + 3 * W_DIFF

kernel_name = 'hybrid_streaming_sb_fox_diff_step'


def rms_norm(x, g):
    xf = x.astype(jnp.float32)
    y = xf * lax.rsqrt(jnp.mean(xf * xf, axis=-1, keepdims=True) + RMS_EPS)
    return (y * g.astype(jnp.float32)).astype(x.dtype)


def split_columns(z):
    parts, start = [], 0
    for size in IN_SIZES:
        parts.append(z[..., start:start + size])
        start += size
    return parts


def project(h, w_in, b_f, fox_q_g, fox_k_g, diff_q_g, diff_k_g):
    b, t, _ = h.shape
    z = jnp.einsum('btd,dn->btn', h, w_in)
    sq, sk, sv, fq, fk, fv, fg, dq, dk, dv = split_columns(z)

    def heads(a, n):
        return a.reshape(b, t, n, -1)

    def qk_norm_diff(a, g):
        return rms_norm(a.reshape(b, t, H_DIFF, 2, DIFF_QK), g).reshape(b, t, H_DIFF, 2 * DIFF_QK)

    fox_logf = jax.nn.log_sigmoid((fg + b_f).astype(jnp.float32))
    return (heads(sq, H_SB), heads(sk, H_SB), heads(sv, H_SB),
            rms_norm(heads(fq, H_FOX), fox_q_g), rms_norm(heads(fk, H_FOX), fox_k_g), heads(fv, H_FOX), fox_logf,
            qk_norm_diff(dq, diff_q_g), qk_norm_diff(dk, diff_k_g), heads(dv, H_DIFF))


def strict_suffix_sum(a):
    n = a.shape[-1]
    if n % KEY_BLOCK == 0:
        blocks = a.reshape(a.shape[:-1] + (n // KEY_BLOCK, KEY_BLOCK))
        tri = jnp.tri(KEY_BLOCK, k=-1, dtype=a.dtype)
        within = jnp.einsum('...nj,js->...ns', blocks, tri)
        tot = jnp.sum(blocks, axis=-1)
        later = lax.cumsum(tot, axis=tot.ndim - 1, reverse=True) - tot
        return (within + later[..., None]).reshape(a.shape)
    return lax.cumsum(a, axis=a.ndim - 1, reverse=True) - a


def stick_breaking(q, k, v, q_pos, k_pos):
    z = jnp.einsum('bqhd,bkhd->bhqk', q, k).astype(jnp.float32) * (HEAD_DIM ** -0.5)
    allowed = k_pos[None, :] < q_pos[:, None]
    log_1m = jnp.where(allowed, jax.nn.log_sigmoid(-z), 0.0)
    after = strict_suffix_sum(log_1m)
    w = jnp.where(allowed, jnp.exp(z + log_1m + after), 0.0)
    return jnp.einsum('bhqk,bkhd->bqhd', w.astype(v.dtype), v)


def forgetting_attention(q, k, v, f_q, f_k, q_pos, k_pos):
    s = jnp.einsum('bqhd,bkhd->bhqk', q, k).astype(jnp.float32) * (HEAD_DIM ** -0.5)
    decay = jnp.swapaxes(f_q, 1, 2)[..., :, None] - jnp.swapaxes(f_k, 1, 2)[..., None, :]
    mask = k_pos[None, :] <= q_pos[:, None]
    p = jax.nn.softmax(jnp.where(mask, s + decay, -jnp.inf), axis=-1)
    return jnp.einsum('bhqk,bkhd->bqhd', p.astype(v.dtype), v)


def diff_attention(q, k, v, lam, slopes, q_pos, k_pos):
    mask = (k_pos // CHUNK)[None, :] <= (q_pos // CHUNK)[:, None]
    dist = jnp.abs(q_pos[:, None] - k_pos[None, :]).astype(jnp.float32)
    bias = jnp.where(mask, -slopes[:, None, None] * dist, -jnp.inf)

    def attn_map(qa, ka):
        s = jnp.einsum('bqhd,bkhd->bhqk', qa, ka).astype(jnp.float32) * (DIFF_QK ** -0.5)
        return jax.nn.softmax(s + bias, axis=-1)

    p = attn_map(q[..., :DIFF_QK], k[..., :DIFF_QK]) - lam * attn_map(q[..., DIFF_QK:], k[..., DIFF_QK:])
    return jnp.einsum('bhqk,bkhd->bqhd', p.astype(v.dtype), v)


def mix_prompt(proj, lam, slopes):
    sb_q, sb_k, sb_v, fox_q, fox_k, fox_v, fox_logf, diff_q, diff_k, diff_v = proj
    seq = sb_q.shape[1]
    cum_f = jnp.cumsum(fox_logf, axis=1)
    o_sb, o_fox, o_diff = [], [], []
    for i in range(seq // Q_BLOCK):
        q0, q1 = i * Q_BLOCK, (i + 1) * Q_BLOCK
        q_pos = jnp.arange(q0, q1, dtype=jnp.int32)
        k_pos = jnp.arange(q1, dtype=jnp.int32)
        o_sb.append(stick_breaking(sb_q[:, q0:q1], sb_k[:, :q1], sb_v[:, :q1], q_pos, k_pos))
        o_fox.append(forgetting_attention(fox_q[:, q0:q1], fox_k[:, :q1], fox_v[:, :q1],
                                          cum_f[:, q0:q1], cum_f[:, :q1], q_pos, k_pos))
        o_diff.append(diff_attention(diff_q[:, q0:q1], diff_k[:, :q1], diff_v[:, :q1], lam, slopes, q_pos, k_pos))
    return (jnp.concatenate(o_sb, axis=1), jnp.concatenate(o_fox, axis=1), jnp.concatenate(o_diff, axis=1))


def mix_sample(proj, c_sb_k, c_sb_v, c_fox_k, c_fox_v, c_fox_logf, c_diff_k, c_diff_v, lam, slopes):
    sb_q, sb_k, sb_v, fox_q, fox_k, fox_v, fox_logf, diff_q, diff_k, diff_v = proj
    past, t = c_sb_k.shape[1], sb_q.shape[1]

    def cat(c, n):
        return jnp.concatenate([c.astype(n.dtype), n], axis=1)

    k_pos = jnp.arange(past + t, dtype=jnp.int32)
    q_pos = past + jnp.arange(t, dtype=jnp.int32)
    cum_f = jnp.cumsum(cat(c_fox_logf, fox_logf), axis=1)
    o_sb = stick_breaking(sb_q, cat(c_sb_k, sb_k), cat(c_sb_v, sb_v), q_pos, k_pos)
    o_fox = forgetting_attention(fox_q, cat(c_fox_k, fox_k), cat(c_fox_v, fox_v), cum_f[:, past:], cum_f, q_pos, k_pos)
    o_diff = diff_attention(diff_q, cat(c_diff_k, diff_k), cat(c_diff_v, diff_v), lam, slopes, q_pos, k_pos)
    return o_sb, o_fox, o_diff


def merge_heads(o_sb, o_fox, o_diff, diff_subln_g, lam_init, w_out):
    b, t = o_sb.shape[:2]
    o_diff = rms_norm(o_diff, diff_subln_g) * (1.0 - lam_init)
    o = jnp.concatenate([o_sb.reshape(b, t, W_SB), o_fox.reshape(b, t, W_FOX),
                         o_diff.reshape(b, t, W_DIFF)], axis=-1)
    return jnp.einsum('btm,md->btd', o, w_out)


def swiglu(h, w_gate, w_up, w_down):
    return jnp.dot(jax.nn.silu(jnp.dot(h, w_gate)) * jnp.dot(h, w_up), w_down)


def moe_ffn(h, w_router, w_gate, w_up, w_down):
    logits = jnp.dot(h, w_router).astype(jnp.float32)
    top_logit, top_idx = lax.top_k(logits, TOP_K)
    gates = jax.nn.softmax(top_logit, axis=-1)
    combine = jnp.einsum('nk,nke->ne', gates,
                         jax.nn.one_hot(top_idx, N_EXPERTS, dtype=jnp.float32)).astype(h.dtype)
    y = jnp.zeros_like(h)
    for e in range(N_EXPERTS):
        y = y + combine[:, e:e + 1] * swiglu(h, w_gate[e], w_up[e], w_down[e])
    return y


def channel_mixer(x, l, norm_g, w_ffn_gate, w_ffn_up, w_ffn_down, w_router, w_moe_gate, w_moe_up, w_moe_down):
    b, t, d = x.shape
    h = rms_norm(x, norm_g).reshape(b * t, d)
    if l % 2 == 0:
        y = swiglu(h, w_ffn_gate[l // 2], w_ffn_up[l // 2], w_ffn_down[l // 2])
    else:
        y = moe_ffn(h, w_router[l // 2], w_moe_gate[l // 2], w_moe_up[l // 2], w_moe_down[l // 2])
    return x + y.reshape(b, t, d)


def setup_inputs(seed: int = 0) -> dict:
    key = jax.random.key(seed)
    ks = list(jax.random.split(key, 32))

    def nrm(shape, scale=1.0):
        return jax.random.normal(ks.pop(), shape, jnp.float32) * scale

    def gain(shape):
        return 1.0 + nrm(shape, 0.02)

    cb = (DEPTH, DEC_BATCH, PAST_LEN)
    return {
        'x_prompt': nrm((BATCH, SEQ, D_MODEL)),
        'x_sample': nrm((DEC_BATCH, DEC_SEQ, D_MODEL)),
        'cache_sb_k': nrm(cb + (H_SB, HEAD_DIM)),
        'cache_sb_v': nrm(cb + (H_SB, HEAD_DIM)),
        'cache_fox_k': nrm(cb + (H_FOX, HEAD_DIM)),
        'cache_fox_v': nrm(cb + (H_FOX, HEAD_DIM)),
        'cache_fox_logf': jax.nn.log_sigmoid(nrm(cb + (H_FOX,))),
        'cache_diff_k': nrm(cb + (H_DIFF, 2 * DIFF_QK)),
        'cache_diff_v': nrm(cb + (H_DIFF, HEAD_DIM)),
        'norm_mix_g': gain((DEPTH, D_MODEL)),
        'norm_ffn_g': gain((DEPTH, D_MODEL)),
        'w_in': nrm((DEPTH, D_MODEL, N_IN), D_MODEL ** -0.5),
        'b_fox_f': nrm((DEPTH, H_FOX), 0.1),
        'fox_q_g': gain((DEPTH, HEAD_DIM)),
        'fox_k_g': gain((DEPTH, HEAD_DIM)),
        'diff_q_g': gain((DEPTH, DIFF_QK)),
        'diff_k_g': gain((DEPTH, DIFF_QK)),
        'diff_lam': nrm((DEPTH, 4, DIFF_QK), 0.1),
        'diff_subln_g': gain((DEPTH, HEAD_DIM)),
        'w_out': nrm((DEPTH, D_MIX, D_MODEL), D_MIX ** -0.5),
        'w_ffn_gate': nrm((N_DENSE, D_MODEL, D_FF), D_MODEL ** -0.5),
        'w_ffn_up': nrm((N_DENSE, D_MODEL, D_FF), D_MODEL ** -0.5),
        'w_ffn_down': nrm((N_DENSE, D_FF, D_MODEL), D_FF ** -0.5),
        'w_router': nrm((N_MOE, D_MODEL, N_EXPERTS), D_MODEL ** -0.5),
        'w_moe_gate': nrm((N_MOE, N_EXPERTS, D_MODEL, D_FF_E), D_MODEL ** -0.5),
        'w_moe_up': nrm((N_MOE, N_EXPERTS, D_MODEL, D_FF_E), D_MODEL ** -0.5),
        'w_moe_down': nrm((N_MOE, N_EXPERTS, D_FF_E, D_MODEL), D_FF_E ** -0.5),
    }


def reference(x_prompt, x_sample, cache_sb_k, cache_sb_v, cache_fox_k, cache_fox_v, cache_fox_logf,
              cache_diff_k, cache_diff_v, norm_mix_g, norm_ffn_g, w_in, b_fox_f, fox_q_g, fox_k_g,
              diff_q_g, diff_k_g, diff_lam, diff_subln_g, w_out, w_ffn_gate, w_ffn_up, w_ffn_down,
              w_router, w_moe_gate, w_moe_up, w_moe_down):
    slopes = jnp.exp2(-8.0 * jnp.arange(1, H_DIFF + 1, dtype=jnp.float32) / H_DIFF)
    xp, xs = x_prompt, x_sample
    rows_p, rows_s = [], []
    for l in range(DEPTH):
        lam_init = 0.8 - 0.6 * math.exp(-0.3 * l)
        lv = diff_lam[l].astype(jnp.float32)
        lam = jnp.exp(jnp.sum(lv[0] * lv[1])) - jnp.exp(jnp.sum(lv[2] * lv[3])) + lam_init
        proj_args = (w_in[l], b_fox_f[l], fox_q_g[l], fox_k_g[l], diff_q_g[l], diff_k_g[l])
        pp = project(rms_norm(xp, norm_mix_g[l]), *proj_args)
        xp = xp + merge_heads(*mix_prompt(pp, lam, slopes), diff_subln_g[l], lam_init, w_out[l])
        rows_p.append((pp[1], pp[2], pp[4], pp[5], pp[6], pp[8], pp[9]))
        ps = project(rms_norm(xs, norm_mix_g[l]), *proj_args)
        os_ = mix_sample(ps, cache_sb_k[l], cache_sb_v[l], cache_fox_k[l], cache_fox_v[l], cache_fox_logf[l],
                         cache_diff_k[l], cache_diff_v[l], lam, slopes)
        xs = xs + merge_heads(*os_, diff_subln_g[l], lam_init, w_out[l])
        rows_s.append((ps[1], ps[2], ps[4], ps[5], ps[6], ps[8], ps[9]))
        ffn_args = (norm_ffn_g[l], w_ffn_gate, w_ffn_up, w_ffn_down, w_router, w_moe_gate, w_moe_up, w_moe_down)
        xp = channel_mixer(xp, l, *ffn_args)
        xs = channel_mixer(xs, l, *ffn_args)
    sb_k_p, sb_v_p, fox_k_p, fox_v_p, fox_logf_p, diff_k_p, diff_v_p = [jnp.stack(r) for r in zip(*rows_p)]
    sb_k_s, sb_v_s, fox_k_s, fox_v_s, fox_logf_s, diff_k_s, diff_v_s = [jnp.stack(r) for r in zip(*rows_s)]
    return (xp, xs, sb_k_p, sb_v_p, fox_k_p, fox_v_p, fox_logf_p, diff_k_p, diff_v_p,
            sb_k_s, sb_v_s, fox_k_s, fox_v_s, fox_logf_s, diff_k_s, diff_v_s)
```

```python
import functools
import math

import jax
import jax.numpy as jnp
from jax import lax
from jax.experimental import pallas as pl
from jax.experimental.pallas import tpu as pltpu

F32 = jnp.float32
BF16 = jnp.bfloat16

HEAD_DIM = 128
N_HEADS = 4
GROUP_W = N_HEADS * HEAD_DIM
DIFF_QK = HEAD_DIM // 2
CHUNK = 64
N_EXPERTS = 8
RMS_EPS = 1e-6
VMEM_LIMIT = 56 * 1024 * 1024
NEG_INF = float("-inf")


def _cparams(*sem):
    return pltpu.CompilerParams(dimension_semantics=sem, vmem_limit_bytes=VMEM_LIMIT)


def _vmem_whole():
    return pl.BlockSpec(memory_space=pltpu.VMEM)


def _rms_rows(x, g):
    return x * lax.rsqrt(jnp.mean(x * x, axis=-1, keepdims=True) + RMS_EPS) * g


def _log_sigmoid(x):
    return -(jnp.maximum(-x, 0.0) + jnp.log1p(jnp.exp(-jnp.abs(x))))


def _dot(a, b):
    return jnp.dot(a, b, preferred_element_type=F32)


def _dot_nt(a, b):
    return lax.dot_general(a, b, (((1,), (1,)), ((), ())), preferred_element_type=F32)


def _head_rms(z, g, group):
    outs = []
    for hh in range(N_HEADS):
        zz = z[:, hh * HEAD_DIM:(hh + 1) * HEAD_DIM]
        sq = zz * zz
        if group == HEAD_DIM:
            r = lax.rsqrt(jnp.mean(sq, axis=-1, keepdims=True) + RMS_EPS)
        else:
            lo = lax.broadcasted_iota(jnp.int32, zz.shape, 1) < DIFF_QK
            s_lo = jnp.sum(jnp.where(lo, sq, 0.0), axis=-1, keepdims=True)
            s_hi = jnp.sum(jnp.where(lo, 0.0, sq), axis=-1, keepdims=True)
            r = jnp.where(lo, lax.rsqrt(s_lo / DIFF_QK + RMS_EPS), lax.rsqrt(s_hi / DIFF_QK + RMS_EPS))
        outs.append(zz * r * g)
    return jnp.concatenate(outs, axis=-1)


def _inproj_kernel(x_ref, g_ref, w_ref, bf_ref, fqg_ref, fkg_ref, dqg_ref, dkg_ref,
                   sbq, sbk, sbv, fxq, fxk, fxv, dfq, dfk, dfv,
                   sbk32, sbv32, fxk32, fxv32, dfk32, dfv32, logf):
    h = _rms_rows(x_ref[...], g_ref[...]).astype(BF16)

    def grp(c):
        return _dot(h, w_ref[:, c * GROUP_W:(c + 1) * GROUP_W])

    z = grp(0)
    sbq[...] = (z * (HEAD_DIM ** -0.5)).astype(BF16)
    z = grp(1)
    sbk32[...] = z
    sbk[...] = z.astype(BF16)
    z = grp(2)
    sbv32[...] = z
    sbv[...] = z.astype(BF16)

    z = _head_rms(grp(3), fqg_ref[...], HEAD_DIM)
    fxq[...] = (z * (HEAD_DIM ** -0.5)).astype(BF16)
    z = _head_rms(grp(4), fkg_ref[...], HEAD_DIM)
    fxk32[...] = z
    fxk[...] = z.astype(BF16)
    z = grp(5)
    fxv32[...] = z
    fxv[...] = z.astype(BF16)

    z = _head_rms(grp(6), dqg_ref[...], DIFF_QK)
    dfq[...] = (z * (DIFF_QK ** -0.5)).astype(BF16)
    z = _head_rms(grp(7), dkg_ref[...], DIFF_QK)
    dfk32[...] = z
    dfk[...] = z.astype(BF16)
    z = grp(8)
    dfv32[...] = z
    dfv[...] = z.astype(BF16)

    zf = _dot(h, w_ref[:, 9 * GROUP_W:9 * GROUP_W + HEAD_DIM])
    logf[...] = _log_sigmoid(zf + bf_ref[...])


def _inproj(x, g, w, bf, fqg, fkg, dqg, dkg, tm):
    t, d = x.shape
    row = lambda i: (i, 0)
    full = lambda i: (0, 0)
    small = pl.BlockSpec((1, HEAD_DIM), full)
    out_bf = jax.ShapeDtypeStruct((t, GROUP_W), BF16)
    out_f = jax.ShapeDtypeStruct((t, GROUP_W), F32)
    slab = pl.BlockSpec((tm, GROUP_W), row)
    return pl.pallas_call(
        _inproj_kernel,
        grid=(t // tm,),
        in_specs=[pl.BlockSpec((tm, d), row), pl.BlockSpec((1, d), full), _vmem_whole(),
                  small, small, small, small, small],
        out_specs=[slab] * 15 + [pl.BlockSpec((tm, HEAD_DIM), row)],
        out_shape=[out_bf] * 9 + [out_f] * 6 + [jax.ShapeDtypeStruct((t, HEAD_DIM), F32)],
        compiler_params=_cparams("arbitrary"),
        name="inproj",
    )(x, g, w, bf, fqg, fkg, dqg, dkg)


def _sb_prompt_kernel(q_ref, k_ref, v_ref, tri_ref, o_ref, c_ref, acc_ref, *, tq, tk):
    i = pl.program_id(1)
    q = q_ref[...]
    ratio = tq // tk
    row = lax.broadcasted_iota(jnp.int32, (tq, tk), 0)
    col = lax.broadcasted_iota(jnp.int32, (tq, tk), 1)
    c_ref[...] = jnp.zeros_like(c_ref)
    acc_ref[...] = jnp.zeros_like(acc_ref)

    def step(j, masked):
        start = pl.multiple_of(j * tk, tk)
        kc = k_ref[pl.ds(start, tk), :]
        vc = v_ref[pl.ds(start, tk), :]
        z = _dot_nt(q, kc)
        log_1m = -(jnp.maximum(z, 0.0) + jnp.log1p(jnp.exp(-jnp.abs(z))))
        if masked:
            allowed = (col + j * tk) < (row + i * tq)
            log_1m = jnp.where(allowed, log_1m, 0.0)
        after = _dot(log_1m.astype(BF16), tri_ref[...]) + c_ref[...]
        w = jnp.exp(z + log_1m + after)
        if masked:
            w = jnp.where(allowed, w, 0.0)
        acc_ref[...] += _dot(w.astype(BF16), vc)
        c_ref[...] += jnp.sum(log_1m, axis=-1, keepdims=True)

    for d in range(ratio):
        step(i * ratio + (ratio - 1 - d), True)

    def body(jj, carry):
        step(i * ratio - 1 - jj, False)
        return carry

    lax.fori_loop(0, i * ratio, body, 0)
    o_ref[...] = acc_ref[...].astype(BF16)


def _sb_prompt(q, k, v, tp, tq, tk):
    tri = jnp.tri(tk, k=-1, dtype=BF16)
    return pl.pallas_call(
        functools.partial(_sb_prompt_kernel, tq=tq, tk=tk),
        grid=(N_HEADS, tp // tq),
        in_specs=[pl.BlockSpec((tq, HEAD_DIM), lambda h, i: (i, h)),
                  pl.BlockSpec((tp, HEAD_DIM), lambda h, i: (0, h)),
                  pl.BlockSpec((tp, HEAD_DIM), lambda h, i: (0, h)),
                  pl.BlockSpec((tk, tk), lambda h, i: (0, 0))],
        out_specs=pl.BlockSpec((tq, HEAD_DIM), lambda h, i: (i, h)),
        out_shape=jax.ShapeDtypeStruct((tp, GROUP_W), BF16),
        scratch_shapes=[pltpu.VMEM((tq, 1), F32), pltpu.VMEM((tq, HEAD_DIM), F32)],
        compiler_params=_cparams("arbitrary", "arbitrary"),
        name="sb_prompt",
    )(q, k, v, tri)


def _softmax_step(s, vc, m_ref, l_ref, acc_ref):
    m_prev = m_ref[...]
    m_new = jnp.maximum(m_prev, jnp.max(s, axis=-1, keepdims=True))
    alpha = jnp.exp(m_prev - m_new)
    p = jnp.exp(s - m_new)
    l_ref[...] = alpha * l_ref[...] + jnp.sum(p, axis=-1, keepdims=True)
    acc_ref[...] = alpha * acc_ref[...] + _dot(p.astype(BF16), vc)
    m_ref[...] = m_new


def _fox_prompt_kernel(q_ref, k_ref, v_ref, fq_ref, fk_ref, o_ref, m_ref, l_ref, acc_ref, *, tq):
    i = pl.program_id(1)
    q = q_ref[...]
    fq = fq_ref[...]
    m_ref[...] = jnp.full_like(m_ref, NEG_INF)
    l_ref[...] = jnp.zeros_like(l_ref)
    acc_ref[...] = jnp.zeros_like(acc_ref)

    def step(j, masked):
        start = pl.multiple_of(j * tq, tq)
        kc = k_ref[pl.ds(start, tq), :]
        vc = v_ref[pl.ds(start, tq), :]
        s = _dot_nt(q, kc) + (fq - fk_ref[j])
        if masked:
            row = lax.broadcasted_iota(jnp.int32, (tq, tq), 0)
            col = lax.broadcasted_iota(jnp.int32, (tq, tq), 1)
            s = jnp.where(col <= row, s, NEG_INF)
        _softmax_step(s, vc, m_ref, l_ref, acc_ref)

    step(i, True)

    def body(j, carry):
        step(j, False)
        return carry

    lax.fori_loop(0, i, body, 0)
    o_ref[...] = (acc_ref[...] * (1.0 / l_ref[...])).astype(BF16)


def _fox_prompt(q, k, v, fq, fk, tp, tq):
    nch = tp // tq
    return pl.pallas_call(
        functools.partial(_fox_prompt_kernel, tq=tq),
        grid=(N_HEADS, nch),
        in_specs=[pl.BlockSpec((tq, HEAD_DIM), lambda h, i: (i, h)),
                  pl.BlockSpec((tp, HEAD_DIM), lambda h, i: (0, h)),
                  pl.BlockSpec((tp, HEAD_DIM), lambda h, i: (0, h)),
                  pl.BlockSpec((None, tq, 1), lambda h, i: (h, i, 0)),
                  pl.BlockSpec((None, nch, 1, tq), lambda h, i: (h, 0, 0, 0))],
        out_specs=pl.BlockSpec((tq, HEAD_DIM), lambda h, i: (i, h)),
        out_shape=jax.ShapeDtypeStruct((tp, GROUP_W), BF16),
        scratch_shapes=[pltpu.VMEM((tq, 1), F32), pltpu.VMEM((tq, 1), F32),
                        pltpu.VMEM((tq, HEAD_DIM), F32)],
        compiler_params=_cparams("arbitrary", "arbitrary"),
        name="fox_prompt",
    )(q, k, v, fq, fk)


def _diff_lambda(lam_ref, lam_init):
    lv = lam_ref[...]
    a = jnp.sum(lv[0:1, :] * lv[1:2, :], axis=-1, keepdims=True)
    b = jnp.sum(lv[2:3, :] * lv[3:4, :], axis=-1, keepdims=True)
    return jnp.exp(a) - jnp.exp(b) + lam_init


def _diff_prompt_kernel(q_ref, k_ref, v_ref, slope_ref, lam_ref, sg_ref, o_ref,
                        m1_ref, l1_ref, a1_ref, m2_ref, l2_ref, a2_ref, *, tq, lam_init):
    i = pl.program_id(1)
    q = q_ref[...]
    first = lax.broadcasted_iota(jnp.int32, q.shape, 1) < DIFF_QK
    zero = jnp.zeros_like(q)
    q1 = jnp.where(first, q, zero)
    q2 = jnp.where(first, zero, q)
    slope = slope_ref[:, 0:1]
    row = lax.broadcasted_iota(jnp.int32, (tq, tq), 0)
    col = lax.broadcasted_iota(jnp.int32, (tq, tq), 1)
    rel = (row - col).astype(F32)
    for m_ref, l_ref, a_ref in ((m1_ref, l1_ref, a1_ref), (m2_ref, l2_ref, a2_ref)):
        m_ref[...] = jnp.full_like(m_ref, NEG_INF)
        l_ref[...] = jnp.zeros_like(l_ref)
        a_ref[...] = jnp.zeros_like(a_ref)

    def step(j, bias):
        start = pl.multiple_of(j * tq, tq)
        kc = k_ref[pl.ds(start, tq), :]
        vc = v_ref[pl.ds(start, tq), :]
        _softmax_step(_dot_nt(q1, kc) + bias, vc, m1_ref, l1_ref, a1_ref)
        _softmax_step(_dot_nt(q2, kc) + bias, vc, m2_ref, l2_ref, a2_ref)

    visible = (col // CHUNK) <= (row // CHUNK)
    step(i, jnp.where(visible, -slope * jnp.abs(rel), NEG_INF))

    bias_rel = -slope * rel

    def body(j, carry):
        step(j, bias_rel - slope * ((i - j) * tq).astype(F32))
        return carry

    lax.fori_loop(0, i, body, 0)
    lam = _diff_lambda(lam_ref, lam_init)
    o = a1_ref[...] * (1.0 / l1_ref[...]) - lam * (a2_ref[...] * (1.0 / l2_ref[...]))
    o_ref[...] = (_rms_rows(o, sg_ref[...]) * (1.0 - lam_init)).astype(BF16)


def _diff_prompt(q, k, v, slopes, lam, subln_g, tp, tq, lam_init):
    col = lambda: pltpu.VMEM((tq, 1), F32)
    acc = lambda: pltpu.VMEM((tq, HEAD_DIM), F32)
    return pl.pallas_call(
        functools.partial(_diff_prompt_kernel, tq=tq, lam_init=lam_init),
        grid=(N_HEADS, tp // tq),
        in_specs=[pl.BlockSpec((tq, HEAD_DIM), lambda h, i: (i, h)),
                  pl.BlockSpec((tp, HEAD_DIM), lambda h, i: (0, h)),
                  pl.BlockSpec((tp, HEAD_DIM), lambda h, i: (0, h)),
                  pl.BlockSpec((None, 1, HEAD_DIM), lambda h, i: (h, 0, 0)),
                  pl.BlockSpec((4, DIFF_QK), lambda h, i: (0, 0)),
                  pl.BlockSpec((1, HEAD_DIM), lambda h, i: (0, 0))],
        out_specs=pl.BlockSpec((tq, HEAD_DIM), lambda h, i: (i, h)),
        out_shape=jax.ShapeDtypeStruct((tp, GROUP_W), BF16),
        scratch_shapes=[col(), col(), acc(), col(), col(), acc()],
        compiler_params=_cparams("arbitrary", "arbitrary"),
        name="diff_prompt",
    )(q, k, v, slopes, lam, subln_g)


def _split_bf16(x):
    hi = x.astype(BF16)
    return hi, (x - hi.astype(F32)).astype(BF16)


def _sb_sample_kernel(q_ref, kn_ref, vn_ref, ck_ref, cv_ref, trip_ref, trin_ref, o_ref, *, past, t):
    rown = lax.broadcasted_iota(jnp.int32, (t, t), 0)
    coln = lax.broadcasted_iota(jnp.int32, (t, t), 1)
    allowed_n = coln < rown
    outs = []
    for hh in range(N_HEADS):
        sl = slice(hh * HEAD_DIM, (hh + 1) * HEAD_DIM)
        q = q_ref[:, sl]
        kp = ck_ref[:, sl].astype(BF16)
        vp = cv_ref[:, sl].astype(BF16)
        zp = _dot_nt(q, kp)
        zn = _dot_nt(q, kn_ref[:, sl])
        lp = _log_sigmoid(-zp)
        ln = jnp.where(allowed_n, _log_sigmoid(-zn), 0.0)
        ln_hi, ln_lo = _split_bf16(ln)
        lp_hi, lp_lo = _split_bf16(lp)
        after_n = _dot(ln_hi, trin_ref[...]) + _dot(ln_lo, trin_ref[...])
        after_p = (_dot(lp_hi, trip_ref[...]) + _dot(lp_lo, trip_ref[...])
                   + jnp.sum(ln, axis=-1, keepdims=True))
        wp = jnp.exp(zp + lp + after_p)
        wn = jnp.where(allowed_n, jnp.exp(zn + ln + after_n), 0.0)
        outs.append(_dot(wp.astype(BF16), vp) + _dot(wn.astype(BF16), vn_ref[:, sl]))
    o_ref[...] = jnp.concatenate(outs, axis=-1).astype(BF16)


def _sample_specs(nb, t, past):
    new = pl.BlockSpec((None, t, GROUP_W), lambda b: (b, 0, 0))
    cache = pl.BlockSpec((None, past, GROUP_W), lambda b: (b, 0, 0))
    return new, cache


def _sb_sample(q, kn, vn, ck, cv):
    nb, t, _ = q.shape
    past = ck.shape[1]
    new, cache = _sample_specs(nb, t, past)
    trip = jnp.tri(past, k=-1, dtype=BF16)
    trin = jnp.tri(t, k=-1, dtype=BF16)
    return pl.pallas_call(
        functools.partial(_sb_sample_kernel, past=past, t=t),
        grid=(nb,),
        in_specs=[new, new, new, cache, cache,
                  pl.BlockSpec((past, past), lambda b: (0, 0)),
                  pl.BlockSpec((t, t), lambda b: (0, 0))],
        out_specs=new,
        out_shape=jax.ShapeDtypeStruct((nb, t, GROUP_W), BF16),
        compiler_params=_cparams("arbitrary"),
        name="sb_sample",
    )(q, kn, vn, ck, cv, trip, trin)


def _fox_sample_kernel(q_ref, kn_ref, vn_ref, ck_ref, cv_ref, fq_ref, fkp_ref, fkn_ref, o_ref, *, t):
    rown = lax.broadcasted_iota(jnp.int32, (t, t), 0)
    coln = lax.broadcasted_iota(jnp.int32, (t, t), 1)
    outs = []
    for hh in range(N_HEADS):
        sl = slice(hh * HEAD_DIM, (hh + 1) * HEAD_DIM)
        q = q_ref[:, sl]
        fq = fq_ref[hh]
        sp = _dot_nt(q, ck_ref[:, sl].astype(BF16)) + (fq - fkp_ref[hh])
        sn = _dot_nt(q, kn_ref[:, sl]) + (fq - fkn_ref[hh])
        sn = jnp.where(coln <= rown, sn, NEG_INF)
        m = jnp.maximum(jnp.max(sp, axis=-1, keepdims=True), jnp.max(sn, axis=-1, keepdims=True))
        pp = jnp.exp(sp - m)
        pn = jnp.exp(sn - m)
        inv = 1.0 / (jnp.sum(pp, axis=-1, keepdims=True) + jnp.sum(pn, axis=-1, keepdims=True))
        outs.append(_dot((pp * inv).astype(BF16), cv_ref[:, sl].astype(BF16))
                    + _dot((pn * inv).astype(BF16), vn_ref[:, sl]))
    o_ref[...] = jnp.concatenate(outs, axis=-1).astype(BF16)


def _fox_sample(q, kn, vn, ck, cv, fq, fkp, fkn):
    nb, t, _ = q.shape
    past = ck.shape[1]
    new, cache = _sample_specs(nb, t, past)
    return pl.pallas_call(
        functools.partial(_fox_sample_kernel, t=t),
        grid=(nb,),
        in_specs=[new, new, new, cache, cache,
                  pl.BlockSpec((None, N_HEADS, t, 1), lambda b: (b, 0, 0, 0)),
                  pl.BlockSpec((None, N_HEADS, 1, past), lambda b: (b, 0, 0, 0)),
                  pl.BlockSpec((None, N_HEADS, 1, t), lambda b: (b, 0, 0, 0))],
        out_specs=new,
        out_shape=jax.ShapeDtypeStruct((nb, t, GROUP_W), BF16),
        compiler_params=_cparams("arbitrary"),
        name="fox_sample",
    )(q, kn, vn, ck, cv, fq, fkp, fkn)


def _diff_sample_kernel(q_ref, kn_ref, vn_ref, ck_ref, cv_ref, slope_ref, lam_ref, sg_ref, o_ref,
                        *, past, t, lam_init):
    lam = _diff_lambda(lam_ref, lam_init)
    q_pos_p = lax.broadcasted_iota(jnp.int32, (t, past), 0) + past
    k_pos_p = lax.broadcasted_iota(jnp.int32, (t, past), 1)
    q_pos_n = lax.broadcasted_iota(jnp.int32, (t, t), 0) + past
    k_pos_n = lax.broadcasted_iota(jnp.int32, (t, t), 1) + past
    vis_p = (k_pos_p // CHUNK) <= (q_pos_p // CHUNK)
    vis_n = (k_pos_n // CHUNK) <= (q_pos_n // CHUNK)
    dist_p = jnp.abs(q_pos_p - k_pos_p).astype(F32)
    dist_n = jnp.abs(q_pos_n - k_pos_n).astype(F32)
    lane = lax.broadcasted_iota(jnp.int32, (t, HEAD_DIM), 1)
    outs = []
    for hh in range(N_HEADS):
        sl = slice(hh * HEAD_DIM, (hh + 1) * HEAD_DIM)
        slope = slope_ref[hh][:, 0:1]
        bias_p = jnp.where(vis_p, -slope * dist_p, NEG_INF)
        bias_n = jnp.where(vis_n, -slope * dist_n, NEG_INF)
        q = q_ref[:, sl]
        zero = jnp.zeros_like(q)
        kp = ck_ref[:, sl].astype(BF16)
        kn = kn_ref[:, sl]
        p_p, p_n = None, None
        for half, coef in ((0, None), (1, lam)):
            qh = jnp.where((lane < DIFF_QK) == (half == 0), q, zero)
            sp = _dot_nt(qh, kp) + bias_p
            sn = _dot_nt(qh, kn) + bias_n
            m = jnp.maximum(jnp.max(sp, axis=-1, keepdims=True), jnp.max(sn, axis=-1, keepdims=True))
            ep = jnp.exp(sp - m)
            en = jnp.exp(sn - m)
            inv = 1.0 / (jnp.sum(ep, axis=-1, keepdims=True) + jnp.sum(en, axis=-1, keepdims=True))
            if coef is None:
                p_p, p_n = ep * inv, en * inv
            else:
                p_p, p_n = p_p - coef * (ep * inv), p_n - coef * (en * inv)
        o = _dot(p_p.astype(BF16), cv_ref[:, sl].astype(BF16)) + _dot(p_n.astype(BF16), vn_ref[:, sl])
        outs.append(_rms_rows(o, sg_ref[...]) * (1.0 - lam_init))
    o_ref[...] = jnp.concatenate(outs, axis=-1).astype(BF16)


def _diff_sample(q, kn, vn, ck, cv, slopes, lam, subln_g, lam_init):
    nb, t, _ = q.shape
    past = ck.shape[1]
    new, cache = _sample_specs(nb, t, past)
    return pl.pallas_call(
        functools.partial(_diff_sample_kernel, past=past, t=t, lam_init=lam_init),
        grid=(nb,),
        in_specs=[new, new, new, cache, cache,
                  pl.BlockSpec((N_HEADS, 1, HEAD_DIM), lambda b: (0, 0, 0)),
                  pl.BlockSpec((4, DIFF_QK), lambda b: (0, 0)),
                  pl.BlockSpec((1, HEAD_DIM), lambda b: (0, 0))],
        out_specs=new,
        out_shape=jax.ShapeDtypeStruct((nb, t, GROUP_W), BF16),
        compiler_params=_cparams("arbitrary"),
        name="diff_sample",
    )(q, kn, vn, ck, cv, slopes, lam, subln_g)


def _outproj_kernel(x_ref, osb_ref, ofx_ref, odf_ref, w_ref, o_ref):
    y = _dot(osb_ref[...], w_ref[0:GROUP_W, :])
    y += _dot(ofx_ref[...], w_ref[GROUP_W:2 * GROUP_W, :])
    y += _dot(odf_ref[...], w_ref[2 * GROUP_W:3 * GROUP_W, :])
    o_ref[...] = x_ref[...] + y


def _outproj(x, o_sb, o_fox, o_diff, w, tm):
    t, d = x.shape
    row = lambda i: (i, 0)
    slab = pl.BlockSpec((tm, GROUP_W), row)
    return pl.pallas_call(
        _outproj_kernel,
        grid=(t // tm,),
        in_specs=[pl.BlockSpec((tm, d), row), slab, slab, slab, _vmem_whole()],
        out_specs=pl.BlockSpec((tm, d), row),
        out_shape=jax.ShapeDtypeStruct((t, d), F32),
        compiler_params=_cparams("arbitrary"),
        name="outproj",
    )(x, o_sb, o_fox, o_diff, w)


def _ffn_kernel(*refs, moe, tiles_per_expert):
    if moe:
        x_ref, g_ref, wg_ref, wu_ref, wd_ref, wr_ref, o_ref, h_ref, comb_ref = refs
    else:
        x_ref, g_ref, wg_ref, wu_ref, wd_ref, o_ref, h_ref = refs
    f = pl.program_id(1)

    @pl.when(f == 0)
    def _():
        hf = _rms_rows(x_ref[...], g_ref[...])
        h_ref[...] = hf.astype(BF16)
        if moe:
            logits = jnp.dot(hf, wr_ref[...], preferred_element_type=F32, precision=lax.Precision.HIGHEST)
            lane = lax.broadcasted_iota(jnp.int32, logits.shape, 1)
            m1 = jnp.max(logits, axis=-1, keepdims=True)
            i1 = jnp.min(jnp.where(logits == m1, lane, N_EXPERTS), axis=-1, keepdims=True)
            rest = jnp.where(lane == i1, NEG_INF, logits)
            m2 = jnp.max(rest, axis=-1, keepdims=True)
            i2 = jnp.min(jnp.where(rest == m2, lane, N_EXPERTS), axis=-1, keepdims=True)
            e2 = jnp.exp(m2 - m1)
            inv = 1.0 / (1.0 + e2)
            comb_ref[...] = jnp.where(lane == i1, inv, 0.0) + jnp.where(lane == i2, e2 * inv, 0.0)

    h = h_ref[...]
    gate = _dot(h, wg_ref[...])
    up = _dot(h, wu_ref[...])
    a = gate * jax.nn.sigmoid(gate) * up
    if moe:
        comb = comb_ref[...]
        lane = lax.broadcasted_iota(jnp.int32, comb.shape, 1)
        a = a * jnp.sum(jnp.where(lane == f // tiles_per_expert, comb, 0.0), axis=-1, keepdims=True)
    part = _dot(a.astype(BF16), wd_ref[...])

    @pl.when(f == 0)
    def _():
        o_ref[...] = x_ref[...] + part

    @pl.when(f != 0)
    def _():
        o_ref[...] += part


def _ffn_dense(x, g, wg, wu, wd, tm, tf):
    t, d = x.shape
    nf = wg.shape[1] // tf
    return pl.pallas_call(
        functools.partial(_ffn_kernel, moe=False, tiles_per_expert=1),
        grid=(t // tm, nf),
        in_specs=[pl.BlockSpec((tm, d), lambda i, f: (i, 0)),
                  pl.BlockSpec((1, d), lambda i, f: (0, 0)),
                  pl.BlockSpec((d, tf), lambda i, f: (0, f)),
                  pl.BlockSpec((d, tf), lambda i, f: (0, f)),
                  pl.BlockSpec((tf, d), lambda i, f: (f, 0))],
        out_specs=pl.BlockSpec((tm, d), lambda i, f: (i, 0)),
        out_shape=jax.ShapeDtypeStruct((t, d), F32),
        scratch_shapes=[pltpu.VMEM((tm, d), BF16)],
        compiler_params=_cparams("arbitrary", "arbitrary"),
        name="ffn_dense",
    )(x, g, wg, wu, wd)


def _ffn_moe(x, g, wg, wu, wd, wr, tm, tf):
    t, d = x.shape
    ne, _, fe = wg.shape
    tpe = fe // tf
    return pl.pallas_call(
        functools.partial(_ffn_kernel, moe=True, tiles_per_expert=tpe),
        grid=(t // tm, ne * tpe),
        in_specs=[pl.BlockSpec((tm, d), lambda i, f: (i, 0)),
                  pl.BlockSpec((1, d), lambda i, f: (0, 0)),
                  pl.BlockSpec((None, d, tf), lambda i, f: (f // tpe, 0, f % tpe)),
                  pl.BlockSpec((None, d, tf), lambda i, f: (f // tpe, 0, f % tpe)),
                  pl.BlockSpec((None, tf, d), lambda i, f: (f // tpe, f % tpe, 0)),
                  pl.BlockSpec((d, ne), lambda i, f: (0, 0))],
        out_specs=pl.BlockSpec((tm, d), lambda i, f: (i, 0)),
        out_shape=jax.ShapeDtypeStruct((t, d), F32),
        scratch_shapes=[pltpu.VMEM((tm, d), BF16), pltpu.VMEM((tm, ne), F32)],
        compiler_params=_cparams("arbitrary", "arbitrary"),
        name="ffn_moe",
    )(x, g, wg, wu, wd, wr)


def _pick_tile(n, prefs):
    for p in prefs:
        if n % p == 0:
            return p
    return n


def _round_up(n, m):
    return (n + m - 1) // m * m


def kernel(x_prompt, x_sample, cache_sb_k, cache_sb_v, cache_fox_k, cache_fox_v, cache_fox_logf, cache_diff_k, cache_diff_v, norm_mix_g, norm_ffn_g, w_in, b_fox_f, fox_q_g, fox_k_g, diff_q_g, diff_k_g, diff_lam, diff_subln_g, w_out, w_ffn_gate, w_ffn_up, w_ffn_down, w_router, w_moe_gate, w_moe_up, w_moe_down):
    bp, tp, d = x_prompt.shape
    nb, ts, _ = x_sample.shape
    depth = w_in.shape[0]
    past = cache_sb_k.shape[2]
    assert bp == 1
    t_all = tp + nb * ts
    tm_in = _pick_tile(t_all, (256, 128, 64, 8))
    tm = _pick_tile(t_all, (512, 256, 128, 64, 8))
    tq = _pick_tile(tp, (512, 256, 128))
    tk_sb = _pick_tile(tq, (256, 128))
    nch = tp // tq

    w3 = 3 * GROUP_W
    w_in_r = jnp.concatenate(
        [w_in[:, :, :2 * w3], w_in[:, :, 2 * w3 + N_HEADS:], w_in[:, :, 2 * w3:2 * w3 + N_HEADS],
         jnp.zeros((depth, d, HEAD_DIM - N_HEADS), w_in.dtype)], axis=-1).astype(BF16)
    b_f = jnp.pad(b_fox_f, ((0, 0), (0, HEAD_DIM - N_HEADS)))
    w_out_b = w_out.astype(BF16)
    w_fg, w_fu, w_fd = w_ffn_gate.astype(BF16), w_ffn_up.astype(BF16), w_ffn_down.astype(BF16)
    ff = w_ffn_gate.shape[-1]
    tf = _pick_tile(ff, (512, 256, 128))
    ffe = w_moe_gate.shape[-1]
    tfe = 512 if ffe >= 512 else 128
    pad_e = _round_up(ffe, tfe) - ffe
    w_mg = jnp.pad(w_moe_gate, ((0, 0), (0, 0), (0, 0), (0, pad_e))).astype(BF16)
    w_mu = jnp.pad(w_moe_up, ((0, 0), (0, 0), (0, 0), (0, pad_e))).astype(BF16)
    w_md = jnp.pad(w_moe_down, ((0, 0), (0, 0), (0, pad_e), (0, 0))).astype(BF16)

    slopes = jnp.exp2(-8.0 * jnp.arange(1, N_HEADS + 1, dtype=F32) / N_HEADS)
    slopes = jnp.broadcast_to(slopes[:, None, None], (N_HEADS, 1, HEAD_DIM))

    def cache2d(c, l):
        return c[l].reshape(nb, past, GROUP_W)

    x = jnp.concatenate([x_prompt.reshape(tp, d), x_sample.reshape(nb * ts, d)], axis=0)
    rows = []
    for l in range(depth):
        lam_init = 0.8 - 0.6 * math.exp(-0.3 * l)
        dqg = jnp.tile(diff_q_g[l], 2)[None, :]
        dkg = jnp.tile(diff_k_g[l], 2)[None, :]
        (sbq, sbk, sbv, fxq, fxk, fxv, dfq, dfk, dfv,
         sbk32, sbv32, fxk32, fxv32, dfk32, dfv32, logf) = _inproj(
            x, norm_mix_g[l][None, :], w_in_r[l], b_f[l][None, :],
            fox_q_g[l][None, :], fox_k_g[l][None, :], dqg, dkg, tm_in)
        logf = logf[:, :N_HEADS]
        rows.append((sbk32, sbv32, fxk32, fxv32, logf, dfk32, dfv32))
        sg = diff_subln_g[l][None, :]
        lam = diff_lam[l]

        cum_p = jnp.cumsum(logf[:tp], axis=0).T
        o_sb_p = _sb_prompt(sbq, sbk, sbv, tp, tq, tk_sb)
        o_fx_p = _fox_prompt(fxq, fxk, fxv, cum_p[:, :, None], cum_p.reshape(N_HEADS, nch, 1, tq), tp, tq)
        o_df_p = _diff_prompt(dfq, dfk, dfv, slopes, lam, sg, tp, tq, lam_init)

        def new(a):
            return a[tp:].reshape(nb, ts, GROUP_W)

        logf_s = logf[tp:].reshape(nb, ts, N_HEADS)
        cum_s = jnp.cumsum(jnp.concatenate([cache_fox_logf[l], logf_s], axis=1), axis=1)
        cum_s = jnp.swapaxes(cum_s, 1, 2)
        o_sb_s = _sb_sample(new(sbq), new(sbk), new(sbv), cache2d(cache_sb_k, l), cache2d(cache_sb_v, l))
        o_fx_s = _fox_sample(new(fxq), new(fxk), new(fxv), cache2d(cache_fox_k, l), cache2d(cache_fox_v, l),
                             cum_s[:, :, past:, None], cum_s[:, :, None, :past], cum_s[:, :, None, past:])
        o_df_s = _diff_sample(new(dfq), new(dfk), new(dfv), cache2d(cache_diff_k, l), cache2d(cache_diff_v, l),
                              slopes, lam, sg, lam_init)

        def both(p, s):
            return jnp.concatenate([p, s.reshape(nb * ts, GROUP_W)], axis=0)

        x = _outproj(x, both(o_sb_p, o_sb_s), both(o_fx_p, o_fx_s), both(o_df_p, o_df_s), w_out_b[l], tm)
        if l % 2 == 0:
            x = _ffn_dense(x, norm_ffn_g[l][None, :], w_fg[l // 2], w_fu[l // 2], w_fd[l // 2], tm, tf)
        else:
            x = _ffn_moe(x, norm_ffn_g[l][None, :], w_mg[l // 2], w_mu[l // 2], w_md[l // 2],
                         w_router[l // 2], tm, tfe)

    def stack(idx, width):
        a = jnp.stack([r[idx] for r in rows])
        tail = (N_HEADS, HEAD_DIM) if width else (N_HEADS,)
        return (a[:, :tp].reshape((depth, 1, tp) + tail), a[:, tp:].reshape((depth, nb, ts) + tail))

    outs_p, outs_s = zip(*[stack(i, i != 4) for i in range(7)])
    return (x[:tp].reshape(1, tp, d), x[tp:].reshape(nb, ts, d)) + tuple(outs_p) + tuple(outs_s)
```

```python
import functools
import math

import jax
import jax.numpy as jnp
from jax import lax
from jax.experimental import pallas as pl
from jax.experimental.pallas import tpu as pltpu

F32 = jnp.float32
BF16 = jnp.bfloat16

LANES = 128
HEAD_DIM = 128
N_HEADS = 4
GROUP_W = N_HEADS * HEAD_DIM
DIFF_QK = HEAD_DIM // 2
CHUNK = 64
CHUNK_SHIFT = 6
N_EXPERTS = 8
RMS_EPS = 1e-6
LOG2E = math.log2(math.e)
VMEM_LIMIT = 56 * 1024 * 1024
SB_TILE = (1024, 256)
FOX_TILE = (1024, 1024)
DIFF_TILE = (1024, 1024)
NEG_INF = float("-inf")
SIGN_BIT = -2 ** 31
HIGHEST = lax.Precision.HIGHEST


def _cparams(*sem):
    return pltpu.CompilerParams(dimension_semantics=sem, vmem_limit_bytes=VMEM_LIMIT)


def _vmem_whole():
    return pl.BlockSpec(memory_space=pltpu.VMEM)


def _any_space():
    return pl.BlockSpec(memory_space=pl.ANY)


def _rms_rows(x, g):
    return x * lax.rsqrt(jnp.mean(x * x, axis=-1, keepdims=True) + RMS_EPS) * g


def _log_sigmoid(x):
    return -(jnp.maximum(-x, 0.0) + jnp.log1p(jnp.exp(-jnp.abs(x))))


def _softplus2(z):
    neg_abs = lax.bitcast_convert_type(lax.bitcast_convert_type(z, jnp.int32) | SIGN_BIT, F32)
    return jnp.maximum(z, 0.0) + jnp.log2(1.0 + jnp.exp2(neg_abs))


def _dot(a, b):
    return jnp.dot(a, b, preferred_element_type=F32)


def _dot_f32(a, b):
    return jnp.dot(a, b, preferred_element_type=F32, precision=HIGHEST)


def _dot_nt(a, b):
    return lax.dot_general(a, b, (((1,), (1,)), ((), ())), preferred_element_type=F32)


def _lane_blocks(x):
    return [x[:, c * LANES:(c + 1) * LANES] for c in range(x.shape[1] // LANES)]


def _sum_list(xs):
    acc = xs[0]
    for x in xs[1:]:
        acc = acc + x
    return acc


def _max_list(xs):
    acc = xs[0]
    for x in xs[1:]:
        acc = jnp.maximum(acc, x)
    return acc


def _head_rms(z, g, group):
    outs = []
    for hh in range(N_HEADS):
        zz = z[:, hh * HEAD_DIM:(hh + 1) * HEAD_DIM]
        sq = zz * zz
        if group == HEAD_DIM:
            r = lax.rsqrt(jnp.mean(sq, axis=-1, keepdims=True) + RMS_EPS)
        else:
            lo = lax.broadcasted_iota(jnp.int32, zz.shape, 1) < DIFF_QK
            s_lo = jnp.sum(jnp.where(lo, sq, 0.0), axis=-1, keepdims=True)
            s_hi = jnp.sum(jnp.where(lo, 0.0, sq), axis=-1, keepdims=True)
            r = jnp.where(lo, lax.rsqrt(s_lo / DIFF_QK + RMS_EPS), lax.rsqrt(s_hi / DIFF_QK + RMS_EPS))
        outs.append(zz * r * g)
    return jnp.concatenate(outs, axis=-1)


N_STACKED = 6


def _inproj_kernel(*refs, with_cumsum, aliased):
    x_ref, g_ref, w_ref, bf_ref, fqg_ref, fkg_ref, dqg_ref, dkg_ref, tri_ref = refs[:9]
    outs = refs[9 + (N_STACKED if aliased else 0):]
    (sbq, sbk, sbv, fxq, fxk, fxv, dfq, dfk, dfv,
     sbk32, sbv32, fxk32, fxv32, dfk32, dfv32, logf, cumf) = outs[:17]
    h = _rms_rows(x_ref[...], g_ref[...]).astype(BF16)

    def grp(c):
        return _dot(h, w_ref[:, c * GROUP_W:(c + 1) * GROUP_W])

    z = grp(0)
    sbq[...] = (z * (HEAD_DIM ** -0.5 * LOG2E)).astype(BF16)
    z = grp(1)
    sbk32[...] = z
    sbk[...] = z.astype(BF16)
    z = grp(2)
    sbv32[...] = z
    sbv[...] = z.astype(BF16)

    z = _head_rms(grp(3), fqg_ref[...], HEAD_DIM)
    fxq[...] = (z * (HEAD_DIM ** -0.5 * LOG2E)).astype(BF16)
    z = _head_rms(grp(4), fkg_ref[...], HEAD_DIM)
    fxk32[...] = z
    fxk[...] = z.astype(BF16)
    z = grp(5)
    fxv32[...] = z
    fxv[...] = z.astype(BF16)

    z = _head_rms(grp(6), dqg_ref[...], DIFF_QK)
    dfq[...] = (z * (DIFF_QK ** -0.5 * LOG2E)).astype(BF16)
    z = _head_rms(grp(7), dkg_ref[...], DIFF_QK)
    dfk32[...] = z
    dfk[...] = z.astype(BF16)
    z = grp(8)
    dfv32[...] = z
    dfv[...] = z.astype(BF16)

    zf = _dot(h, w_ref[:, 9 * GROUP_W:9 * GROUP_W + HEAD_DIM])
    lf = _log_sigmoid(zf + bf_ref[...])
    logf[...] = lf
    if with_cumsum:
        carry_ref = outs[17]

        @pl.when(pl.program_id(0) == 0)
        def _():
            carry_ref[...] = jnp.zeros_like(carry_ref)

        cum = _dot_f32(tri_ref[...], lf) + carry_ref[0:1, :]
        cumf[...] = cum
        tm = cum.shape[0]
        carry_ref[...] = jnp.broadcast_to(cum[tm - 1:tm, :], carry_ref.shape)
    else:
        cumf[...] = lf


def _inproj(x, tile0, rows, layer, depth, stacked, g, w, bf, fqg, fkg, dqg, dkg, tm, with_cumsum):
    d = x.shape[1]
    aliased = layer > 0
    row = lambda i: (i, 0)
    full = lambda i: (0, 0)
    small = pl.BlockSpec((1, HEAD_DIM), full)
    slab = pl.BlockSpec((tm, GROUP_W), row)
    narrow = pl.BlockSpec((tm, HEAD_DIM), row)
    stacked_spec = pl.BlockSpec((None, tm, GROUP_W), lambda i: (layer, i, 0))
    tri = jnp.tri(tm, dtype=F32)
    in_specs = [pl.BlockSpec((tm, d), lambda i: (i + tile0, 0)), pl.BlockSpec((1, d), full), _vmem_whole(),
                small, small, small, small, small, pl.BlockSpec((tm, tm), full)]
    args = [x, g, w, bf, fqg, fkg, dqg, dkg, tri]
    aliases = {}
    if aliased:
        in_specs += [_any_space()] * N_STACKED
        aliases = {len(args) + k: 9 + k for k in range(N_STACKED)}
        args += list(stacked)
    out_bf = jax.ShapeDtypeStruct((rows, GROUP_W), BF16)
    out_n = jax.ShapeDtypeStruct((rows, HEAD_DIM), F32)
    out_st = jax.ShapeDtypeStruct((depth, rows, GROUP_W), F32)
    return pl.pallas_call(
        functools.partial(_inproj_kernel, with_cumsum=with_cumsum, aliased=aliased),
        grid=(rows // tm,),
        in_specs=in_specs,
        out_specs=[slab] * 9 + [stacked_spec] * N_STACKED + [narrow, narrow],
        out_shape=[out_bf] * 9 + [out_st] * N_STACKED + [out_n, out_n],
        scratch_shapes=[pltpu.VMEM((8, HEAD_DIM), F32)] if with_cumsum else [],
        input_output_aliases=aliases,
        compiler_params=_cparams("arbitrary"),
        name="inproj",
    )(*args)


def _sb_prompt_kernel(q_ref, k_ref, v_ref, tri_ref, o_ref, c_ref, acc_ref, *, tq, tk):
    i = pl.program_id(1)
    q = q_ref[...]
    ratio = tq // tk
    row = lax.broadcasted_iota(jnp.int32, (tq, tk), 0)
    col = lax.broadcasted_iota(jnp.int32, (tq, tk), 1)
    c_ref[...] = jnp.zeros_like(c_ref)
    acc_ref[...] = jnp.zeros_like(acc_ref)

    def step(j, masked):
        kc, vc = _kv_chunk(k_ref, v_ref, j, tk)
        z = _dot_nt(q, kc)
        cost = _softplus2(z)
        if masked:
            allowed = (col + j * tk) < (row + i * tq)
            cost = jnp.where(allowed, cost, 0.0)
        incl = _dot(cost.astype(BF16), tri_ref[...])
        w = jnp.exp2(z - incl)
        if masked:
            w = jnp.where(allowed, w, 0.0)
        carried = c_ref[...]
        acc_ref[...] += jnp.exp2(-carried) * _dot(w.astype(BF16), vc)
        c_ref[...] = carried + incl[:, 0:1]

    for d in range(ratio):
        step(i * ratio + (ratio - 1 - d), True)

    def body(jj, carry):
        step(i * ratio - 1 - jj, False)
        return carry

    lax.fori_loop(0, i * ratio, body, 0)
    o_ref[...] = acc_ref[...].astype(BF16)


def _sb_prompt(q, k, v, tp, tq, tk):
    tri = jnp.tri(tk, dtype=BF16)
    return pl.pallas_call(
        functools.partial(_sb_prompt_kernel, tq=tq, tk=tk),
        grid=(N_HEADS, tp // tq),
        in_specs=[pl.BlockSpec((tq, HEAD_DIM), lambda h, i: (i, h)),
                  _kv_spec(tp), _kv_spec(tp),
                  pl.BlockSpec((tk, tk), lambda h, i: (0, 0))],
        out_specs=pl.BlockSpec((tq, HEAD_DIM), lambda h, i: (i, h)),
        out_shape=jax.ShapeDtypeStruct((tp, GROUP_W), BF16),
        scratch_shapes=[pltpu.VMEM((tq, LANES), F32), pltpu.VMEM((tq, HEAD_DIM), F32)],
        compiler_params=_cparams("arbitrary", "arbitrary"),
        name="sb_prompt",
    )(q, k, v, tri)


def _softmax_step(s, vc, m_ref, l_ref, acc_ref):
    blocks = _lane_blocks(s)
    m_prev = m_ref[...]
    m_new = jnp.maximum(m_prev, jnp.max(_max_list(blocks), axis=-1, keepdims=True))
    alpha = jnp.exp2(m_prev - m_new)
    ps = [jnp.exp2(b - m_new) for b in blocks]
    l_ref[...] = alpha * l_ref[...] + _sum_list(ps)
    acc_ref[...] = alpha * acc_ref[...] + _dot(jnp.concatenate(ps, axis=1).astype(BF16), vc)
    m_ref[...] = m_new


def _softmax_init(m_ref, l_ref, acc_ref):
    m_ref[...] = jnp.full_like(m_ref, NEG_INF)
    l_ref[...] = jnp.zeros_like(l_ref)
    acc_ref[...] = jnp.zeros_like(acc_ref)


def _softmax_result(l_ref, acc_ref):
    return acc_ref[...] * (1.0 / jnp.sum(l_ref[...], axis=-1, keepdims=True))


def _diag_chunks(i, tq, tk):
    if tk >= tq:
        n_full = (i * tq) // tk
        return n_full, [n_full]
    ratio = tq // tk
    return i * ratio, [i * ratio + d for d in range(ratio)]


def _kv_chunk(k_ref, v_ref, j, tk):
    start = pl.multiple_of(j * tk, tk)
    return k_ref[pl.ds(start, tk), :], v_ref[pl.ds(start, tk), :]


def _fox_prompt_kernel(q_ref, k_ref, v_ref, fk_ref, o_ref, m_ref, l_ref, acc_ref, *, tq, tk):
    i = pl.program_id(1)
    q = q_ref[...]
    n_full, diag = _diag_chunks(i, tq, tk)
    f0 = fk_ref[diag[0]][:, 0:1]
    _softmax_init(m_ref, l_ref, acc_ref)

    def step(j, masked):
        kc, vc = _kv_chunk(k_ref, v_ref, j, tk)
        s = _dot_nt(q, kc) + (f0 - fk_ref[j]) * LOG2E
        if masked:
            row = lax.broadcasted_iota(jnp.int32, (tq, tk), 0) + i * tq
            col = lax.broadcasted_iota(jnp.int32, (tq, tk), 1) + j * tk
            s = jnp.where(col <= row, s, NEG_INF)
        _softmax_step(s, vc, m_ref, l_ref, acc_ref)

    for j in diag:
        step(j, True)

    def body(j, carry):
        step(j, False)
        return carry

    lax.fori_loop(0, n_full, body, 0)
    o_ref[...] = _softmax_result(l_ref, acc_ref).astype(BF16)


def _kv_spec(tp):
    return pl.BlockSpec((tp, HEAD_DIM), lambda h, i: (0, h), pipeline_mode=pl.Buffered(1))


def _fox_prompt(q, k, v, fk, tp, tq, tk):
    stat = lambda: pltpu.VMEM((tq, LANES), F32)
    return pl.pallas_call(
        functools.partial(_fox_prompt_kernel, tq=tq, tk=tk),
        grid=(N_HEADS, tp // tq),
        in_specs=[pl.BlockSpec((tq, HEAD_DIM), lambda h, i: (i, h)),
                  _kv_spec(tp), _kv_spec(tp),
                  pl.BlockSpec((None, tp // tk, 1, tk), lambda h, i: (h, 0, 0, 0))],
        out_specs=pl.BlockSpec((tq, HEAD_DIM), lambda h, i: (i, h)),
        out_shape=jax.ShapeDtypeStruct((tp, GROUP_W), BF16),
        scratch_shapes=[stat(), stat(), pltpu.VMEM((tq, HEAD_DIM), F32)],
        compiler_params=_cparams("arbitrary", "arbitrary"),
        name="fox_prompt",
    )(q, k, v, fk)


def _diff_lambda(lam_ref, lam_init):
    lv = lam_ref[...]
    a = jnp.sum(lv[0:1, :] * lv[1:2, :], axis=-1, keepdims=True)
    b = jnp.sum(lv[2:3, :] * lv[3:4, :], axis=-1, keepdims=True)
    return jnp.exp(a) - jnp.exp(b) + lam_init


def _split_halves(q):
    first = lax.broadcasted_iota(jnp.int32, q.shape, 1) < DIFF_QK
    zero = jnp.zeros_like(q)
    return jnp.where(first, q, zero), jnp.where(first, zero, q)


def _diff_prompt_kernel(q_ref, k_ref, v_ref, slope_ref, lam_ref, sg_ref, o_ref,
                        m1_ref, l1_ref, a1_ref, m2_ref, l2_ref, a2_ref, *, tq, tk, lam_init):
    i = pl.program_id(1)
    q1, q2 = _split_halves(q_ref[...])
    slope = slope_ref[:, 0:1] * LOG2E
    n_full, diag = _diag_chunks(i, tq, tk)
    _softmax_init(m1_ref, l1_ref, a1_ref)
    _softmax_init(m2_ref, l2_ref, a2_ref)

    def step(j, bias):
        kc, vc = _kv_chunk(k_ref, v_ref, j, tk)
        _softmax_step(_dot_nt(q1, kc) + bias, vc, m1_ref, l1_ref, a1_ref)
        _softmax_step(_dot_nt(q2, kc) + bias, vc, m2_ref, l2_ref, a2_ref)

    for j in diag:
        t = lax.broadcasted_iota(jnp.int32, (tq, tk), 0) + i * tq
        s = lax.broadcasted_iota(jnp.int32, (tq, tk), 1) + j * tk
        visible = lax.shift_right_logical(s, CHUNK_SHIFT) <= lax.shift_right_logical(t, CHUNK_SHIFT)
        rel = (t - i * tq - jnp.abs(t - s)).astype(F32)
        step(j, jnp.where(visible, slope * rel, NEG_INF))

    col_row = lax.broadcasted_iota(jnp.int32, (1, tk), 1)

    def body(j, carry):
        step(j, slope * (col_row + (j * tk - i * tq)).astype(F32))
        return carry

    lax.fori_loop(0, n_full, body, 0)
    lam = _diff_lambda(lam_ref, lam_init)
    o = _softmax_result(l1_ref, a1_ref) - lam * _softmax_result(l2_ref, a2_ref)
    o_ref[...] = (_rms_rows(o, sg_ref[...]) * (1.0 - lam_init)).astype(BF16)


def _diff_prompt(q, k, v, slopes, lam, subln_g, tp, tq, tk, lam_init):
    stat = lambda: pltpu.VMEM((tq, LANES), F32)
    acc = lambda: pltpu.VMEM((tq, HEAD_DIM), F32)
    return pl.pallas_call(
        functools.partial(_diff_prompt_kernel, tq=tq, tk=tk, lam_init=lam_init),
        grid=(N_HEADS, tp // tq),
        in_specs=[pl.BlockSpec((tq, HEAD_DIM), lambda h, i: (i, h)),
                  _kv_spec(tp), _kv_spec(tp),
                  pl.BlockSpec((None, 1, HEAD_DIM), lambda h, i: (h, 0, 0)),
                  pl.BlockSpec((4, DIFF_QK), lambda h, i: (0, 0)),
                  pl.BlockSpec((1, HEAD_DIM), lambda h, i: (0, 0))],
        out_specs=pl.BlockSpec((tq, HEAD_DIM), lambda h, i: (i, h)),
        out_shape=jax.ShapeDtypeStruct((tp, GROUP_W), BF16),
        scratch_shapes=[stat(), stat(), acc(), stat(), stat(), acc()],
        compiler_params=_cparams("arbitrary", "arbitrary"),
        name="diff_prompt",
    )(q, k, v, slopes, lam, subln_g)


def _split_bf16(x):
    hi = x.astype(BF16)
    return hi, (x - hi.astype(F32)).astype(BF16)


def _sb_sample_kernel(q_ref, kn_ref, vn_ref, ck_ref, cv_ref, trip_ref, trin_ref, o_ref, *, past, t):
    rown = lax.broadcasted_iota(jnp.int32, (t, t), 0)
    coln = lax.broadcasted_iota(jnp.int32, (t, t), 1)
    allowed_n = coln < rown
    outs = []
    for hh in range(N_HEADS):
        sl = slice(hh * HEAD_DIM, (hh + 1) * HEAD_DIM)
        q = q_ref[:, sl]
        kp = ck_ref[:, sl].astype(BF16)
        vp = cv_ref[:, sl].astype(BF16)
        zp = _dot_nt(q, kp)
        zn = _dot_nt(q, kn_ref[:, sl])
        cp = _softplus2(zp)
        cn = jnp.where(allowed_n, _softplus2(zn), 0.0)
        cn_hi, cn_lo = _split_bf16(cn)
        cp_hi, cp_lo = _split_bf16(cp)
        later_n = _dot(cn_hi, trin_ref[...]) + _dot(cn_lo, trin_ref[...])
        later_p = (_dot(cp_hi, trip_ref[...]) + _dot(cp_lo, trip_ref[...])
                   + jnp.sum(cn, axis=-1, keepdims=True))
        wp = jnp.exp2(zp - cp - later_p)
        wn = jnp.where(allowed_n, jnp.exp2(zn - cn - later_n), 0.0)
        outs.append(_dot(wp.astype(BF16), vp) + _dot(wn.astype(BF16), vn_ref[:, sl]))
    o_ref[...] = jnp.concatenate(outs, axis=-1).astype(BF16)


def _sample_specs(t, past):
    new = pl.BlockSpec((None, t, GROUP_W), lambda b: (b, 0, 0))
    cache = pl.BlockSpec((None, past, GROUP_W), lambda b: (b, 0, 0))
    return new, cache


def _sb_sample(q, kn, vn, ck, cv):
    nb, t, _ = q.shape
    past = ck.shape[1]
    new, cache = _sample_specs(t, past)
    trip = jnp.tri(past, k=-1, dtype=BF16)
    trin = jnp.tri(t, k=-1, dtype=BF16)
    return pl.pallas_call(
        functools.partial(_sb_sample_kernel, past=past, t=t),
        grid=(nb,),
        in_specs=[new, new, new, cache, cache,
                  pl.BlockSpec((past, past), lambda b: (0, 0)),
                  pl.BlockSpec((t, t), lambda b: (0, 0))],
        out_specs=new,
        out_shape=jax.ShapeDtypeStruct((nb, t, GROUP_W), BF16),
        compiler_params=_cparams("arbitrary"),
        name="sb_sample",
    )(q, kn, vn, ck, cv, trip, trin)


def _fox_sample_kernel(q_ref, kn_ref, vn_ref, ck_ref, cv_ref, lfp_ref, lfn_ref, trip_ref, trin_ref, o_ref,
                       *, past, t):
    rown = lax.broadcasted_iota(jnp.int32, (t, t), 0)
    coln = lax.broadcasted_iota(jnp.int32, (t, t), 1)
    cum_p = _dot_f32(lfp_ref[...], trip_ref[...])
    total = cum_p[:, past - 1:past]
    bias_p = (total - cum_p) * LOG2E
    bias_n = -_dot_f32(lfn_ref[...], trin_ref[...]) * LOG2E
    outs = []
    for hh in range(N_HEADS):
        sl = slice(hh * HEAD_DIM, (hh + 1) * HEAD_DIM)
        q = q_ref[:, sl]
        sp = _dot_nt(q, ck_ref[:, sl].astype(BF16)) + bias_p[hh:hh + 1, :]
        sn = _dot_nt(q, kn_ref[:, sl]) + bias_n[hh:hh + 1, :]
        sn = jnp.where(coln <= rown, sn, NEG_INF)
        m = jnp.maximum(jnp.max(sp, axis=-1, keepdims=True), jnp.max(sn, axis=-1, keepdims=True))
        pp = jnp.exp2(sp - m)
        pn = jnp.exp2(sn - m)
        inv = 1.0 / (jnp.sum(pp, axis=-1, keepdims=True) + jnp.sum(pn, axis=-1, keepdims=True))
        outs.append(_dot((pp * inv).astype(BF16), cv_ref[:, sl].astype(BF16))
                    + _dot((pn * inv).astype(BF16), vn_ref[:, sl]))
    o_ref[...] = jnp.concatenate(outs, axis=-1).astype(BF16)


def _fox_sample(q, kn, vn, ck, cv, lf_past, lf_new):
    nb, t, _ = q.shape
    past = ck.shape[1]
    new, cache = _sample_specs(t, past)
    trip = jnp.tri(past, dtype=F32).T
    trin = jnp.tri(t, dtype=F32).T
    return pl.pallas_call(
        functools.partial(_fox_sample_kernel, past=past, t=t),
        grid=(nb,),
        in_specs=[new, new, new, cache, cache,
                  pl.BlockSpec((None, 8, past), lambda b: (b, 0, 0)),
                  pl.BlockSpec((None, 8, t), lambda b: (b, 0, 0)),
                  pl.BlockSpec((past, past), lambda b: (0, 0)),
                  pl.BlockSpec((t, t), lambda b: (0, 0))],
        out_specs=new,
        out_shape=jax.ShapeDtypeStruct((nb, t, GROUP_W), BF16),
        compiler_params=_cparams("arbitrary"),
        name="fox_sample",
    )(q, kn, vn, ck, cv, lf_past, lf_new, trip, trin)


def _diff_sample_kernel(q_ref, kn_ref, vn_ref, ck_ref, cv_ref, slope_ref, lam_ref, sg_ref, o_ref,
                        *, past, t, lam_init):
    lam = _diff_lambda(lam_ref, lam_init)
    q_pos_p = lax.broadcasted_iota(jnp.int32, (t, past), 0) + past
    k_pos_p = lax.broadcasted_iota(jnp.int32, (t, past), 1)
    q_pos_n = lax.broadcasted_iota(jnp.int32, (t, t), 0) + past
    k_pos_n = lax.broadcasted_iota(jnp.int32, (t, t), 1) + past
    vis_p = lax.shift_right_logical(k_pos_p, CHUNK_SHIFT) <= lax.shift_right_logical(q_pos_p, CHUNK_SHIFT)
    vis_n = lax.shift_right_logical(k_pos_n, CHUNK_SHIFT) <= lax.shift_right_logical(q_pos_n, CHUNK_SHIFT)
    dist_p = jnp.abs(q_pos_p - k_pos_p).astype(F32)
    dist_n = jnp.abs(q_pos_n - k_pos_n).astype(F32)
    outs = []
    for hh in range(N_HEADS):
        sl = slice(hh * HEAD_DIM, (hh + 1) * HEAD_DIM)
        slope = slope_ref[hh][:, 0:1] * LOG2E
        bias_p = jnp.where(vis_p, -slope * dist_p, NEG_INF)
        bias_n = jnp.where(vis_n, -slope * dist_n, NEG_INF)
        kp = ck_ref[:, sl].astype(BF16)
        kn = kn_ref[:, sl]
        p_p, p_n = None, None
        for qh, coef in zip(_split_halves(q_ref[:, sl]), (None, lam)):
            sp = _dot_nt(qh, kp) + bias_p
            sn = _dot_nt(qh, kn) + bias_n
            m = jnp.maximum(jnp.max(sp, axis=-1, keepdims=True), jnp.max(sn, axis=-1, keepdims=True))
            ep = jnp.exp2(sp - m)
            en = jnp.exp2(sn - m)
            inv = 1.0 / (jnp.sum(ep, axis=-1, keepdims=True) + jnp.sum(en, axis=-1, keepdims=True))
            if coef is None:
                p_p, p_n = ep * inv, en * inv
            else:
                p_p, p_n = p_p - coef * (ep * inv), p_n - coef * (en * inv)
        o = _dot(p_p.astype(BF16), cv_ref[:, sl].astype(BF16)) + _dot(p_n.astype(BF16), vn_ref[:, sl])
        outs.append(_rms_rows(o, sg_ref[...]) * (1.0 - lam_init))
    o_ref[...] = jnp.concatenate(outs, axis=-1).astype(BF16)


def _diff_sample(q, kn, vn, ck, cv, slopes, lam, subln_g, lam_init):
    nb, t, _ = q.shape
    past = ck.shape[1]
    new, cache = _sample_specs(t, past)
    return pl.pallas_call(
        functools.partial(_diff_sample_kernel, past=past, t=t, lam_init=lam_init),
        grid=(nb,),
        in_specs=[new, new, new, cache, cache,
                  pl.BlockSpec((N_HEADS, 1, HEAD_DIM), lambda b: (0, 0, 0)),
                  pl.BlockSpec((4, DIFF_QK), lambda b: (0, 0)),
                  pl.BlockSpec((1, HEAD_DIM), lambda b: (0, 0))],
        out_specs=new,
        out_shape=jax.ShapeDtypeStruct((nb, t, GROUP_W), BF16),
        compiler_params=_cparams("arbitrary"),
        name="diff_sample",
    )(q, kn, vn, ck, cv, slopes, lam, subln_g)


def _outproj_kernel(x_ref, osb_ref, ofx_ref, odf_ref, w_ref, o_ref):
    y = _dot(osb_ref[...], w_ref[0:GROUP_W, :])
    y += _dot(ofx_ref[...], w_ref[GROUP_W:2 * GROUP_W, :])
    y += _dot(odf_ref[...], w_ref[2 * GROUP_W:3 * GROUP_W, :])
    o_ref[...] = x_ref[...] + y


def _outproj(x, tile0, o_sb, o_fox, o_diff, w, tm):
    rows = o_sb.shape[0]
    d = x.shape[1]
    row = lambda i: (i, 0)
    xrow = lambda i: (i + tile0, 0)
    slab = pl.BlockSpec((tm, GROUP_W), row)
    return pl.pallas_call(
        _outproj_kernel,
        grid=(rows // tm,),
        in_specs=[pl.BlockSpec((tm, d), xrow), slab, slab, slab, _vmem_whole()],
        out_specs=pl.BlockSpec((tm, d), xrow),
        out_shape=jax.ShapeDtypeStruct(x.shape, F32),
        input_output_aliases={0: 0},
        compiler_params=_cparams("arbitrary"),
        name="outproj",
    )(x, o_sb, o_fox, o_diff, w)


def _ffn_kernel(*refs, moe, tiles_per_expert):
    if moe:
        x_ref, g_ref, wg_ref, wu_ref, wd_ref, wr_ref, o_ref, h_ref, comb_ref = refs
    else:
        x_ref, g_ref, wg_ref, wu_ref, wd_ref, o_ref, h_ref = refs
    f = pl.program_id(1)

    @pl.when(f == 0)
    def _():
        hf = _rms_rows(x_ref[...], g_ref[...])
        h_ref[...] = hf.astype(BF16)
        if moe:
            logits = _dot_f32(hf, wr_ref[...])
            lane = lax.broadcasted_iota(jnp.int32, logits.shape, 1)
            m1 = jnp.max(logits, axis=-1, keepdims=True)
            i1 = jnp.min(jnp.where(logits == m1, lane, N_EXPERTS), axis=-1, keepdims=True)
            rest = jnp.where(lane == i1, NEG_INF, logits)
            m2 = jnp.max(rest, axis=-1, keepdims=True)
            i2 = jnp.min(jnp.where(rest == m2, lane, N_EXPERTS), axis=-1, keepdims=True)
            e2 = jnp.exp(m2 - m1)
            inv = 1.0 / (1.0 + e2)
            comb_ref[...] = jnp.where(lane == i1, inv, 0.0) + jnp.where(lane == i2, e2 * inv, 0.0)

    h = h_ref[...]
    gate = _dot(h, wg_ref[...])
    up = _dot(h, wu_ref[...])
    a = gate * jax.nn.sigmoid(gate) * up
    if moe:
        comb = comb_ref[...]
        lane = lax.broadcasted_iota(jnp.int32, comb.shape, 1)
        a = a * jnp.sum(jnp.where(lane == f // tiles_per_expert, comb, 0.0), axis=-1, keepdims=True)
    part = _dot(a.astype(BF16), wd_ref[...])

    @pl.when(f == 0)
    def _():
        o_ref[...] = x_ref[...] + part

    @pl.when(f != 0)
    def _():
        o_ref[...] += part


def _ffn_dense(x, g, wg, wu, wd, tm, tf):
    t, d = x.shape
    nf = wg.shape[1] // tf
    return pl.pallas_call(
        functools.partial(_ffn_kernel, moe=False, tiles_per_expert=1),
        grid=(t // tm, nf),
        in_specs=[pl.BlockSpec((tm, d), lambda i, f: (i, 0)),
                  pl.BlockSpec((1, d), lambda i, f: (0, 0)),
                  pl.BlockSpec((d, tf), lambda i, f: (0, f)),
                  pl.BlockSpec((d, tf), lambda i, f: (0, f)),
                  pl.BlockSpec((tf, d), lambda i, f: (f, 0))],
        out_specs=pl.BlockSpec((tm, d), lambda i, f: (i, 0)),
        out_shape=jax.ShapeDtypeStruct((t, d), F32),
        scratch_shapes=[pltpu.VMEM((tm, d), BF16)],
        compiler_params=_cparams("arbitrary", "arbitrary"),
        name="ffn_dense",
    )(x, g, wg, wu, wd)


def _ffn_moe(x, g, wg, wu, wd, wr, tm, tf):
    t, d = x.shape
    ne, _, fe = wg.shape
    tpe = fe // tf
    return pl.pallas_call(
        functools.partial(_ffn_kernel, moe=True, tiles_per_expert=tpe),
        grid=(t // tm, ne * tpe),
        in_specs=[pl.BlockSpec((tm, d), lambda i, f: (i, 0)),
                  pl.BlockSpec((1, d), lambda i, f: (0, 0)),
                  pl.BlockSpec((None, d, tf), lambda i, f: (f // tpe, 0, f % tpe)),
                  pl.BlockSpec((None, d, tf), lambda i, f: (f // tpe, 0, f % tpe)),
                  pl.BlockSpec((None, tf, d), lambda i, f: (f // tpe, f % tpe, 0)),
                  pl.BlockSpec((d, ne), lambda i, f: (0, 0))],
        out_specs=pl.BlockSpec((tm, d), lambda i, f: (i, 0)),
        out_shape=jax.ShapeDtypeStruct((t, d), F32),
        scratch_shapes=[pltpu.VMEM((tm, d), BF16), pltpu.VMEM((tm, ne), F32)],
        compiler_params=_cparams("arbitrary", "arbitrary"),
        name="ffn_moe",
    )(x, g, wg, wu, wd, wr)


def _pick_tile(n, prefs):
    for p in prefs:
        if n % p == 0:
            return p
    return n


def _round_up(n, m):
    return (n + m - 1) // m * m


def kernel(x_prompt, x_sample, cache_sb_k, cache_sb_v, cache_fox_k, cache_fox_v, cache_fox_logf, cache_diff_k, cache_diff_v, norm_mix_g, norm_ffn_g, w_in, b_fox_f, fox_q_g, fox_k_g, diff_q_g, diff_k_g, diff_lam, diff_subln_g, w_out, w_ffn_gate, w_ffn_up, w_ffn_down, w_router, w_moe_gate, w_moe_up, w_moe_down):
    bp, tp, d = x_prompt.shape
    nb, ts, _ = x_sample.shape
    depth = w_in.shape[0]
    past = cache_sb_k.shape[2]
    assert bp == 1
    n_s = nb * ts
    t_all = tp + n_s
    tm_in = math.gcd(256, math.gcd(tp, n_s))
    tm_out = math.gcd(512, math.gcd(tp, n_s))
    tm = _pick_tile(t_all, (512, 256, 128, 64, 8))
    tq_sb, tk_sb = _pick_tile(tp, SB_TILE[:1] + (256, 128)), _pick_tile(tp, SB_TILE[1:] + (128,))
    tq_fx, tk_fx = _pick_tile(tp, FOX_TILE[:1] + (256, 128)), _pick_tile(tp, FOX_TILE[1:] + (256, 128))
    tq_df, tk_df = _pick_tile(tp, DIFF_TILE[:1] + (256, 128)), _pick_tile(tp, DIFF_TILE[1:] + (256, 128))

    w3 = 3 * GROUP_W
    w_in_r = jnp.concatenate(
        [w_in[:, :, :2 * w3], w_in[:, :, 2 * w3 + N_HEADS:], w_in[:, :, 2 * w3:2 * w3 + N_HEADS],
         jnp.zeros((depth, d, HEAD_DIM - N_HEADS), w_in.dtype)], axis=-1).astype(BF16)
    b_f = jnp.pad(b_fox_f, ((0, 0), (0, HEAD_DIM - N_HEADS)))
    w_out_b = w_out.astype(BF16)
    w_fg, w_fu, w_fd = w_ffn_gate.astype(BF16), w_ffn_up.astype(BF16), w_ffn_down.astype(BF16)
    ff = w_ffn_gate.shape[-1]
    tf = _pick_tile(ff, (512, 256, 128))
    ffe = w_moe_gate.shape[-1]
    tfe = 512 if ffe >= 512 else 128
    pad_e = _round_up(ffe, tfe) - ffe
    w_mg = jnp.pad(w_moe_gate, ((0, 0), (0, 0), (0, 0), (0, pad_e))).astype(BF16)
    w_mu = jnp.pad(w_moe_up, ((0, 0), (0, 0), (0, 0), (0, pad_e))).astype(BF16)
    w_md = jnp.pad(w_moe_down, ((0, 0), (0, 0), (0, pad_e), (0, 0))).astype(BF16)

    slopes = jnp.exp2(-8.0 * jnp.arange(1, N_HEADS + 1, dtype=F32) / N_HEADS)
    slopes = jnp.broadcast_to(slopes[:, None, None], (N_HEADS, 1, HEAD_DIM))

    def cache2d(c, l):
        return c[l].reshape(nb, past, GROUP_W)

    x = jnp.concatenate([x_prompt.reshape(tp, d), x_sample.reshape(n_s, d)], axis=0)
    stacked_p, stacked_s = (), ()
    logf_p, logf_s = [], []
    for l in range(depth):
        lam_init = 0.8 - 0.6 * math.exp(-0.3 * l)
        dqg = jnp.tile(diff_q_g[l], 2)[None, :]
        dkg = jnp.tile(diff_k_g[l], 2)[None, :]
        proj_args = (norm_mix_g[l][None, :], w_in_r[l], b_f[l][None, :],
                     fox_q_g[l][None, :], fox_k_g[l][None, :], dqg, dkg, tm_in)
        sg = diff_subln_g[l][None, :]
        lam = diff_lam[l]

        res = _inproj(x, 0, tp, l, depth, stacked_p, *proj_args, True)
        sbq, sbk, sbv, fxq, fxk, fxv, dfq, dfk, dfv = res[:9]
        stacked_p = tuple(res[9:9 + N_STACKED])
        logf_p.append(res[15][:, :N_HEADS])
        cum_k = res[16][:, :N_HEADS].T.reshape(N_HEADS, tp // tk_fx, 1, tk_fx)
        o_sb = _sb_prompt(sbq, sbk, sbv, tp, tq_sb, tk_sb)
        o_fx = _fox_prompt(fxq, fxk, fxv, cum_k, tp, tq_fx, tk_fx)
        o_df = _diff_prompt(dfq, dfk, dfv, slopes, lam, sg, tp, tq_df, tk_df, lam_init)
        x = _outproj(x, 0, o_sb, o_fx, o_df, w_out_b[l], tm_out)

        res = _inproj(x, tp // tm_in, n_s, l, depth, stacked_s, *proj_args, False)
        sbq, sbk, sbv, fxq, fxk, fxv, dfq, dfk, dfv = [a.reshape(nb, ts, GROUP_W) for a in res[:9]]
        stacked_s = tuple(res[9:9 + N_STACKED])
        lf_s = res[15][:, :N_HEADS]
        logf_s.append(lf_s)
        lf_new = jnp.pad(jnp.swapaxes(lf_s.reshape(nb, ts, N_HEADS), 1, 2), ((0, 0), (0, 8 - N_HEADS), (0, 0)))
        lf_past = jnp.pad(jnp.swapaxes(cache_fox_logf[l], 1, 2), ((0, 0), (0, 8 - N_HEADS), (0, 0)))
        o_sb = _sb_sample(sbq, sbk, sbv, cache2d(cache_sb_k, l), cache2d(cache_sb_v, l))
        o_fx = _fox_sample(fxq, fxk, fxv, cache2d(cache_fox_k, l), cache2d(cache_fox_v, l), lf_past, lf_new)
        o_df = _diff_sample(dfq, dfk, dfv, cache2d(cache_diff_k, l), cache2d(cache_diff_v, l),
                            slopes, lam, sg, lam_init)
        x = _outproj(x, tp // tm_out, o_sb.reshape(n_s, GROUP_W), o_fx.reshape(n_s, GROUP_W),
                     o_df.reshape(n_s, GROUP_W), w_out_b[l], tm_out)

        if l % 2 == 0:
            x = _ffn_dense(x, norm_ffn_g[l][None, :], w_fg[l // 2], w_fu[l // 2], w_fd[l // 2], tm, tf)
        else:
            x = _ffn_moe(x, norm_ffn_g[l][None, :], w_mg[l // 2], w_mu[l // 2], w_md[l // 2],
                         w_router[l // 2], tm, tfe)

    kv_tail = (N_HEADS, HEAD_DIM)
    sbk_p, sbv_p, fxk_p, fxv_p, dfk_p, dfv_p = [a.reshape((depth, 1, tp) + kv_tail) for a in stacked_p]
    sbk_s, sbv_s, fxk_s, fxv_s, dfk_s, dfv_s = [a.reshape((depth, nb, ts) + kv_tail) for a in stacked_s]
    lf_p = jnp.stack(logf_p).reshape(depth, 1, tp, N_HEADS)
    lf_s = jnp.stack(logf_s).reshape(depth, nb, ts, N_HEADS)
    return (x[:tp].reshape(1, tp, d), x[tp:].reshape(nb, ts, d),
            sbk_p, sbv_p, fxk_p, fxv_p, lf_p, dfk_p, dfv_p,
            sbk_s, sbv_s, fxk_s, fxv_s, lf_s, dfk_s, dfv_s)
```

```python
import functools
import math

import jax
import jax.numpy as jnp
from jax import lax
from jax.experimental import pallas as pl
from jax.experimental.pallas import tpu as pltpu

F32 = jnp.float32
BF16 = jnp.bfloat16

LANES = 128
HEAD_DIM = 128
N_HEADS = 4
GROUP_W = N_HEADS * HEAD_DIM
DIFF_QK = HEAD_DIM // 2
CHUNK = 64
CHUNK_SHIFT = 6
N_EXPERTS = 8
RMS_EPS = 1e-6
LOG2E = math.log2(math.e)
VMEM_LIMIT = 56 * 1024 * 1024
SB_TILE = (1024, 256)
FOX_TILE = (1024, 1024)
DIFF_TILE = (1024, 1024)
NEG_INF = float("-inf")
SIGN_BIT = -2 ** 31
HIGHEST = lax.Precision.HIGHEST


def _cparams(*sem):
    return pltpu.CompilerParams(dimension_semantics=sem, vmem_limit_bytes=VMEM_LIMIT)


def _vmem_whole():
    return pl.BlockSpec(memory_space=pltpu.VMEM)


def _any_space():
    return pl.BlockSpec(memory_space=pl.ANY)


def _rms_rows(x, g):
    return x * lax.rsqrt(jnp.mean(x * x, axis=-1, keepdims=True) + RMS_EPS) * g


def _log_sigmoid(x):
    return -(jnp.maximum(-x, 0.0) + jnp.log1p(jnp.exp(-jnp.abs(x))))


def _softplus2(z):
    neg_abs = lax.bitcast_convert_type(lax.bitcast_convert_type(z, jnp.int32) | SIGN_BIT, F32)
    return jnp.maximum(z, 0.0) + jnp.log2(1.0 + jnp.exp2(neg_abs))


def _dot(a, b):
    return jnp.dot(a, b, preferred_element_type=F32)


def _dot_f32(a, b):
    return jnp.dot(a, b, preferred_element_type=F32, precision=HIGHEST)


def _dot_nt(a, b):
    return lax.dot_general(a, b, (((1,), (1,)), ((), ())), preferred_element_type=F32)


def _lane_blocks(x):
    return [x[:, c * LANES:(c + 1) * LANES] for c in range(x.shape[1] // LANES)]


def _sum_list(xs):
    acc = xs[0]
    for x in xs[1:]:
        acc = acc + x
    return acc


def _max_list(xs):
    acc = xs[0]
    for x in xs[1:]:
        acc = jnp.maximum(acc, x)
    return acc


def _head_rms(z, g, group):
    outs = []
    for hh in range(N_HEADS):
        zz = z[:, hh * HEAD_DIM:(hh + 1) * HEAD_DIM]
        sq = zz * zz
        if group == HEAD_DIM:
            r = lax.rsqrt(jnp.mean(sq, axis=-1, keepdims=True) + RMS_EPS)
        else:
            lo = lax.broadcasted_iota(jnp.int32, zz.shape, 1) < DIFF_QK
            s_lo = jnp.sum(jnp.where(lo, sq, 0.0), axis=-1, keepdims=True)
            s_hi = jnp.sum(jnp.where(lo, 0.0, sq), axis=-1, keepdims=True)
            r = jnp.where(lo, lax.rsqrt(s_lo / DIFF_QK + RMS_EPS), lax.rsqrt(s_hi / DIFF_QK + RMS_EPS))
        outs.append(zz * r * g)
    return jnp.concatenate(outs, axis=-1)


N_STACKED = 6


def _store_heads(o_ref, z):
    rows = z.shape[0]
    for hh in range(N_HEADS):
        o_ref[pl.ds(hh, rows, stride=N_HEADS), :] = z[:, hh * HEAD_DIM:(hh + 1) * HEAD_DIM]


def _inproj_kernel(*refs, with_cumsum, aliased):
    x_ref, g_ref, w_ref, bf_ref, fqg_ref, fkg_ref, dqg_ref, dkg_ref, tri_ref = refs[:9]
    outs = refs[9 + (N_STACKED if aliased else 0):]
    (sbq, sbk, sbv, fxq, fxk, fxv, dfq, dfk, dfv,
     sbk32, sbv32, fxk32, fxv32, dfk32, dfv32, logf, cumf) = outs[:17]
    h = _rms_rows(x_ref[...], g_ref[...]).astype(BF16)

    def grp(c):
        return _dot(h, w_ref[:, c * GROUP_W:(c + 1) * GROUP_W])

    z = grp(0)
    sbq[...] = (z * (HEAD_DIM ** -0.5 * LOG2E)).astype(BF16)
    z = grp(1)
    _store_heads(sbk32, z)
    sbk[...] = z.astype(BF16)
    z = grp(2)
    _store_heads(sbv32, z)
    sbv[...] = z.astype(BF16)

    z = _head_rms(grp(3), fqg_ref[...], HEAD_DIM)
    fxq[...] = (z * (HEAD_DIM ** -0.5 * LOG2E)).astype(BF16)
    z = _head_rms(grp(4), fkg_ref[...], HEAD_DIM)
    _store_heads(fxk32, z)
    fxk[...] = z.astype(BF16)
    z = grp(5)
    _store_heads(fxv32, z)
    fxv[...] = z.astype(BF16)

    z = _head_rms(grp(6), dqg_ref[...], DIFF_QK)
    dfq[...] = (z * (DIFF_QK ** -0.5 * LOG2E)).astype(BF16)
    z = _head_rms(grp(7), dkg_ref[...], DIFF_QK)
    _store_heads(dfk32, z)
    dfk[...] = z.astype(BF16)
    z = grp(8)
    _store_heads(dfv32, z)
    dfv[...] = z.astype(BF16)

    zf = _dot(h, w_ref[:, 9 * GROUP_W:9 * GROUP_W + HEAD_DIM])
    lf = _log_sigmoid(zf + bf_ref[...])
    logf[...] = lf
    if with_cumsum:
        carry_ref = outs[17]

        @pl.when(pl.program_id(0) == 0)
        def _():
            carry_ref[...] = jnp.zeros_like(carry_ref)

        cum = _dot_f32(tri_ref[...], lf) + carry_ref[0:1, :]
        cumf[...] = cum
        tm = cum.shape[0]
        carry_ref[...] = jnp.broadcast_to(cum[tm - 1:tm, :], carry_ref.shape)
    else:
        cumf[...] = lf


def _inproj(x, tile0, rows, layer, depth, stacked, g, w, bf, fqg, fkg, dqg, dkg, tm, with_cumsum):
    d = x.shape[1]
    aliased = layer > 0
    row = lambda i: (i, 0)
    full = lambda i: (0, 0)
    small = pl.BlockSpec((1, HEAD_DIM), full)
    slab = pl.BlockSpec((tm, GROUP_W), row)
    narrow = pl.BlockSpec((tm, HEAD_DIM), row)
    stacked_spec = pl.BlockSpec((None, tm * N_HEADS, HEAD_DIM), lambda i: (layer, i, 0))
    tri = jnp.tri(tm, dtype=F32)
    in_specs = [pl.BlockSpec((tm, d), lambda i: (i + tile0, 0)), pl.BlockSpec((1, d), full), _vmem_whole(),
                small, small, small, small, small, pl.BlockSpec((tm, tm), full)]
    args = [x, g, w, bf, fqg, fkg, dqg, dkg, tri]
    aliases = {}
    if aliased:
        in_specs += [_any_space()] * N_STACKED
        aliases = {len(args) + k: 9 + k for k in range(N_STACKED)}
        args += list(stacked)
    out_bf = jax.ShapeDtypeStruct((rows, GROUP_W), BF16)
    out_n = jax.ShapeDtypeStruct((rows, HEAD_DIM), F32)
    out_st = jax.ShapeDtypeStruct((depth, rows * N_HEADS, HEAD_DIM), F32)
    return pl.pallas_call(
        functools.partial(_inproj_kernel, with_cumsum=with_cumsum, aliased=aliased),
        grid=(rows // tm,),
        in_specs=in_specs,
        out_specs=[slab] * 9 + [stacked_spec] * N_STACKED + [narrow, narrow],
        out_shape=[out_bf] * 9 + [out_st] * N_STACKED + [out_n, out_n],
        scratch_shapes=[pltpu.VMEM((8, HEAD_DIM), F32)] if with_cumsum else [],
        input_output_aliases=aliases,
        compiler_params=_cparams("arbitrary"),
        name="inproj",
    )(*args)


def _sb_prompt_kernel(q_ref, k_ref, v_ref, tri_ref, o_ref, c_ref, acc_ref, *, tq, tk):
    i = pl.program_id(1)
    q = q_ref[...]
    ratio = tq // tk
    row = lax.broadcasted_iota(jnp.int32, (tq, tk), 0)
    col = lax.broadcasted_iota(jnp.int32, (tq, tk), 1)
    c_ref[...] = jnp.zeros_like(c_ref)
    acc_ref[...] = jnp.zeros_like(acc_ref)

    def step(j, masked):
        kc, vc = _kv_chunk(k_ref, v_ref, j, tk)
        z = _dot_nt(q, kc)
        cost = _softplus2(z)
        if masked:
            allowed = (col + j * tk) < (row + i * tq)
            cost = jnp.where(allowed, cost, 0.0)
        incl = _dot(cost.astype(BF16), tri_ref[...])
        w = jnp.exp2(z - incl)
        if masked:
            w = jnp.where(allowed, w, 0.0)
        carried = c_ref[...]
        acc_ref[...] += jnp.exp2(-carried) * _dot(w.astype(BF16), vc)
        c_ref[...] = carried + incl[:, 0:1]

    for d in range(ratio):
        step(i * ratio + (ratio - 1 - d), True)

    def body(jj, carry):
        step(i * ratio - 1 - jj, False)
        return carry

    lax.fori_loop(0, i * ratio, body, 0)
    o_ref[...] = acc_ref[...].astype(BF16)


def _sb_prompt(q, k, v, tp, tq, tk):
    tri = jnp.tri(tk, dtype=BF16)
    return pl.pallas_call(
        functools.partial(_sb_prompt_kernel, tq=tq, tk=tk),
        grid=(N_HEADS, tp // tq),
        in_specs=[pl.BlockSpec((tq, HEAD_DIM), lambda h, i: (i, h)),
                  _kv_spec(tp), _kv_spec(tp),
                  pl.BlockSpec((tk, tk), lambda h, i: (0, 0))],
        out_specs=pl.BlockSpec((tq, HEAD_DIM), lambda h, i: (i, h)),
        out_shape=jax.ShapeDtypeStruct((tp, GROUP_W), BF16),
        scratch_shapes=[pltpu.VMEM((tq, LANES), F32), pltpu.VMEM((tq, HEAD_DIM), F32)],
        compiler_params=_cparams("arbitrary", "arbitrary"),
        name="sb_prompt",
    )(q, k, v, tri)


def _softmax_step(s, vc, m_ref, l_ref, acc_ref):
    blocks = _lane_blocks(s)
    m_prev = m_ref[...]
    m_new = jnp.maximum(m_prev, jnp.max(_max_list(blocks), axis=-1, keepdims=True))
    alpha = jnp.exp2(m_prev - m_new)
    ps = [jnp.exp2(b - m_new) for b in blocks]
    l_ref[...] = alpha * l_ref[...] + _sum_list(ps)
    acc_ref[...] = alpha * acc_ref[...] + _dot(jnp.concatenate(ps, axis=1).astype(BF16), vc)
    m_ref[...] = m_new


def _softmax_init(m_ref, l_ref, acc_ref):
    m_ref[...] = jnp.full_like(m_ref, NEG_INF)
    l_ref[...] = jnp.zeros_like(l_ref)
    acc_ref[...] = jnp.zeros_like(acc_ref)


def _softmax_result(l_ref, acc_ref):
    return acc_ref[...] * (1.0 / jnp.sum(l_ref[...], axis=-1, keepdims=True))


def _diag_chunks(i, tq, tk):
    if tk >= tq:
        n_full = (i * tq) // tk
        return n_full, [n_full]
    ratio = tq // tk
    return i * ratio, [i * ratio + d for d in range(ratio)]


def _kv_chunk(k_ref, v_ref, j, tk):
    start = pl.multiple_of(j * tk, tk)
    return k_ref[pl.ds(start, tk), :], v_ref[pl.ds(start, tk), :]


def _fox_prompt_kernel(q_ref, k_ref, v_ref, fk_ref, o_ref, m_ref, l_ref, acc_ref, *, tq, tk):
    i = pl.program_id(1)
    q = q_ref[...]
    n_full, diag = _diag_chunks(i, tq, tk)
    f0 = fk_ref[diag[0]][:, 0:1]
    _softmax_init(m_ref, l_ref, acc_ref)

    def step(j, masked):
        kc, vc = _kv_chunk(k_ref, v_ref, j, tk)
        s = _dot_nt(q, kc) + (f0 - fk_ref[j]) * LOG2E
        if masked:
            row = lax.broadcasted_iota(jnp.int32, (tq, tk), 0) + i * tq
            col = lax.broadcasted_iota(jnp.int32, (tq, tk), 1) + j * tk
            s = jnp.where(col <= row, s, NEG_INF)
        _softmax_step(s, vc, m_ref, l_ref, acc_ref)

    for j in diag:
        step(j, True)

    def body(j, carry):
        step(j, False)
        return carry

    lax.fori_loop(0, n_full, body, 0)
    o_ref[...] = _softmax_result(l_ref, acc_ref).astype(BF16)


def _kv_spec(tp):
    return pl.BlockSpec((tp, HEAD_DIM), lambda h, i: (0, h), pipeline_mode=pl.Buffered(1))


def _fox_prompt(q, k, v, fk, tp, tq, tk):
    stat = lambda: pltpu.VMEM((tq, LANES), F32)
    return pl.pallas_call(
        functools.partial(_fox_prompt_kernel, tq=tq, tk=tk),
        grid=(N_HEADS, tp // tq),
        in_specs=[pl.BlockSpec((tq, HEAD_DIM), lambda h, i: (i, h)),
                  _kv_spec(tp), _kv_spec(tp),
                  pl.BlockSpec((None, tp // tk, 1, tk), lambda h, i: (h, 0, 0, 0))],
        out_specs=pl.BlockSpec((tq, HEAD_DIM), lambda h, i: (i, h)),
        out_shape=jax.ShapeDtypeStruct((tp, GROUP_W), BF16),
        scratch_shapes=[stat(), stat(), pltpu.VMEM((tq, HEAD_DIM), F32)],
        compiler_params=_cparams("arbitrary", "arbitrary"),
        name="fox_prompt",
    )(q, k, v, fk)


def _diff_lambda(lam_ref, lam_init):
    lv = lam_ref[...]
    a = jnp.sum(lv[0:1, :] * lv[1:2, :], axis=-1, keepdims=True)
    b = jnp.sum(lv[2:3, :] * lv[3:4, :], axis=-1, keepdims=True)
    return jnp.exp(a) - jnp.exp(b) + lam_init


def _split_halves(q):
    first = lax.broadcasted_iota(jnp.int32, q.shape, 1) < DIFF_QK
    zero = jnp.zeros_like(q)
    return jnp.where(first, q, zero), jnp.where(first, zero, q)


def _diff_prompt_kernel(q_ref, k_ref, v_ref, slope_ref, lam_ref, sg_ref, o_ref,
                        m1_ref, l1_ref, a1_ref, m2_ref, l2_ref, a2_ref, *, tq, tk, lam_init):
    i = pl.program_id(1)
    q1, q2 = _split_halves(q_ref[...])
    slope = slope_ref[:, 0:1] * LOG2E
    n_full, diag = _diag_chunks(i, tq, tk)
    _softmax_init(m1_ref, l1_ref, a1_ref)
    _softmax_init(m2_ref, l2_ref, a2_ref)

    def step(j, bias):
        kc, vc = _kv_chunk(k_ref, v_ref, j, tk)
        _softmax_step(_dot_nt(q1, kc) + bias, vc, m1_ref, l1_ref, a1_ref)
        _softmax_step(_dot_nt(q2, kc) + bias, vc, m2_ref, l2_ref, a2_ref)

    for j in diag:
        t = lax.broadcasted_iota(jnp.int32, (tq, tk), 0) + i * tq
        s = lax.broadcasted_iota(jnp.int32, (tq, tk), 1) + j * tk
        visible = lax.shift_right_logical(s, CHUNK_SHIFT) <= lax.shift_right_logical(t, CHUNK_SHIFT)
        rel = (t - i * tq - jnp.abs(t - s)).astype(F32)
        step(j, jnp.where(visible, slope * rel, NEG_INF))

    col_row = lax.broadcasted_iota(jnp.int32, (1, tk), 1)

    def body(j, carry):
        step(j, slope * (col_row + (j * tk - i * tq)).astype(F32))
        return carry

    lax.fori_loop(0, n_full, body, 0)
    lam = _diff_lambda(lam_ref, lam_init)
    o = _softmax_result(l1_ref, a1_ref) - lam * _softmax_result(l2_ref, a2_ref)
    o_ref[...] = (_rms_rows(o, sg_ref[...]) * (1.0 - lam_init)).astype(BF16)


def _diff_prompt(q, k, v, slopes, lam, subln_g, tp, tq, tk, lam_init):
    stat = lambda: pltpu.VMEM((tq, LANES), F32)
    acc = lambda: pltpu.VMEM((tq, HEAD_DIM), F32)
    return pl.pallas_call(
        functools.partial(_diff_prompt_kernel, tq=tq, tk=tk, lam_init=lam_init),
        grid=(N_HEADS, tp // tq),
        in_specs=[pl.BlockSpec((tq, HEAD_DIM), lambda h, i: (i, h)),
                  _kv_spec(tp), _kv_spec(tp),
                  pl.BlockSpec((None, 1, HEAD_DIM), lambda h, i: (h, 0, 0)),
                  pl.BlockSpec((4, DIFF_QK), lambda h, i: (0, 0)),
                  pl.BlockSpec((1, HEAD_DIM), lambda h, i: (0, 0))],
        out_specs=pl.BlockSpec((tq, HEAD_DIM), lambda h, i: (i, h)),
        out_shape=jax.ShapeDtypeStruct((tp, GROUP_W), BF16),
        scratch_shapes=[stat(), stat(), acc(), stat(), stat(), acc()],
        compiler_params=_cparams("arbitrary", "arbitrary"),
        name="diff_prompt",
    )(q, k, v, slopes, lam, subln_g)


def _split_bf16(x):
    hi = x.astype(BF16)
    return hi, (x - hi.astype(F32)).astype(BF16)


def _sb_sample_kernel(q_ref, kn_ref, vn_ref, ck_ref, cv_ref, trip_ref, trin_ref, o_ref, *, past, t):
    rown = lax.broadcasted_iota(jnp.int32, (t, t), 0)
    coln = lax.broadcasted_iota(jnp.int32, (t, t), 1)
    allowed_n = coln < rown
    outs = []
    for hh in range(N_HEADS):
        sl = slice(hh * HEAD_DIM, (hh + 1) * HEAD_DIM)
        q = q_ref[:, sl]
        kp = _cache_head(ck_ref, hh)
        vp = _cache_head(cv_ref, hh)
        zp = _dot_nt(q, kp)
        zn = _dot_nt(q, kn_ref[:, sl])
        cp = _softplus2(zp)
        cn = jnp.where(allowed_n, _softplus2(zn), 0.0)
        cn_hi, cn_lo = _split_bf16(cn)
        cp_hi, cp_lo = _split_bf16(cp)
        later_n = _dot(cn_hi, trin_ref[...]) + _dot(cn_lo, trin_ref[...])
        later_p = (_dot(cp_hi, trip_ref[...]) + _dot(cp_lo, trip_ref[...])
                   + jnp.sum(cn, axis=-1, keepdims=True))
        wp = jnp.exp2(zp - cp - later_p)
        wn = jnp.where(allowed_n, jnp.exp2(zn - cn - later_n), 0.0)
        outs.append(_dot(wp.astype(BF16), vp) + _dot(wn.astype(BF16), vn_ref[:, sl]))
    o_ref[...] = jnp.concatenate(outs, axis=-1).astype(BF16)


def _cache_head(c_ref, hh):
    past = c_ref.shape[0] // N_HEADS
    return c_ref[pl.ds(hh, past, stride=N_HEADS), :].astype(BF16)


def _sample_specs(t, past, layer):
    new = pl.BlockSpec((None, t, GROUP_W), lambda b: (b, 0, 0))
    cache = pl.BlockSpec((None, None, past * N_HEADS, HEAD_DIM), lambda b: (layer, b, 0, 0))
    return new, cache


def _sb_sample(q, kn, vn, ck, cv, layer):
    nb, t, _ = q.shape
    past = ck.shape[2] // N_HEADS
    new, cache = _sample_specs(t, past, layer)
    trip = jnp.tri(past, k=-1, dtype=BF16)
    trin = jnp.tri(t, k=-1, dtype=BF16)
    return pl.pallas_call(
        functools.partial(_sb_sample_kernel, past=past, t=t),
        grid=(nb,),
        in_specs=[new, new, new, cache, cache,
                  pl.BlockSpec((past, past), lambda b: (0, 0)),
                  pl.BlockSpec((t, t), lambda b: (0, 0))],
        out_specs=new,
        out_shape=jax.ShapeDtypeStruct((nb, t, GROUP_W), BF16),
        compiler_params=_cparams("arbitrary"),
        name="sb_sample",
    )(q, kn, vn, ck, cv, trip, trin)


def _fox_sample_kernel(q_ref, kn_ref, vn_ref, ck_ref, cv_ref, lfp_ref, lfn_ref, trip_ref, trin_ref, o_ref,
                       *, past, t):
    rown = lax.broadcasted_iota(jnp.int32, (t, t), 0)
    coln = lax.broadcasted_iota(jnp.int32, (t, t), 1)
    cum_p = _dot_f32(lfp_ref[...], trip_ref[...])
    total = cum_p[:, past - 1:past]
    bias_p = (total - cum_p) * LOG2E
    bias_n = -_dot_f32(lfn_ref[...], trin_ref[...]) * LOG2E
    outs = []
    for hh in range(N_HEADS):
        sl = slice(hh * HEAD_DIM, (hh + 1) * HEAD_DIM)
        q = q_ref[:, sl]
        sp = _dot_nt(q, _cache_head(ck_ref, hh)) + bias_p[hh:hh + 1, :]
        sn = _dot_nt(q, kn_ref[:, sl]) + bias_n[hh:hh + 1, :]
        sn = jnp.where(coln <= rown, sn, NEG_INF)
        m = jnp.maximum(jnp.max(sp, axis=-1, keepdims=True), jnp.max(sn, axis=-1, keepdims=True))
        pp = jnp.exp2(sp - m)
        pn = jnp.exp2(sn - m)
        inv = 1.0 / (jnp.sum(pp, axis=-1, keepdims=True) + jnp.sum(pn, axis=-1, keepdims=True))
        outs.append(_dot((pp * inv).astype(BF16), _cache_head(cv_ref, hh))
                    + _dot((pn * inv).astype(BF16), vn_ref[:, sl]))
    o_ref[...] = jnp.concatenate(outs, axis=-1).astype(BF16)


def _fox_sample(q, kn, vn, ck, cv, lf_past, lf_new, layer):
    nb, t, _ = q.shape
    past = ck.shape[2] // N_HEADS
    new, cache = _sample_specs(t, past, layer)
    trip = jnp.tri(past, dtype=F32).T
    trin = jnp.tri(t, dtype=F32).T
    return pl.pallas_call(
        functools.partial(_fox_sample_kernel, past=past, t=t),
        grid=(nb,),
        in_specs=[new, new, new, cache, cache,
                  pl.BlockSpec((None, 8, past), lambda b: (b, 0, 0)),
                  pl.BlockSpec((None, 8, t), lambda b: (b, 0, 0)),
                  pl.BlockSpec((past, past), lambda b: (0, 0)),
                  pl.BlockSpec((t, t), lambda b: (0, 0))],
        out_specs=new,
        out_shape=jax.ShapeDtypeStruct((nb, t, GROUP_W), BF16),
        compiler_params=_cparams("arbitrary"),
        name="fox_sample",
    )(q, kn, vn, ck, cv, lf_past, lf_new, trip, trin)


def _diff_sample_kernel(q_ref, kn_ref, vn_ref, ck_ref, cv_ref, slope_ref, lam_ref, sg_ref, o_ref,
                        *, past, t, lam_init):
    lam = _diff_lambda(lam_ref, lam_init)
    q_pos_p = lax.broadcasted_iota(jnp.int32, (t, past), 0) + past
    k_pos_p = lax.broadcasted_iota(jnp.int32, (t, past), 1)
    q_pos_n = lax.broadcasted_iota(jnp.int32, (t, t), 0) + past
    k_pos_n = lax.broadcasted_iota(jnp.int32, (t, t), 1) + past
    vis_p = lax.shift_right_logical(k_pos_p, CHUNK_SHIFT) <= lax.shift_right_logical(q_pos_p, CHUNK_SHIFT)
    vis_n = lax.shift_right_logical(k_pos_n, CHUNK_SHIFT) <= lax.shift_right_logical(q_pos_n, CHUNK_SHIFT)
    dist_p = jnp.abs(q_pos_p - k_pos_p).astype(F32)
    dist_n = jnp.abs(q_pos_n - k_pos_n).astype(F32)
    outs = []
    for hh in range(N_HEADS):
        sl = slice(hh * HEAD_DIM, (hh + 1) * HEAD_DIM)
        slope = slope_ref[hh][:, 0:1] * LOG2E
        bias_p = jnp.where(vis_p, -slope * dist_p, NEG_INF)
        bias_n = jnp.where(vis_n, -slope * dist_n, NEG_INF)
        kp = _cache_head(ck_ref, hh)
        kn = kn_ref[:, sl]
        p_p, p_n = None, None
        for qh, coef in zip(_split_halves(q_ref[:, sl]), (None, lam)):
            sp = _dot_nt(qh, kp) + bias_p
            sn = _dot_nt(qh, kn) + bias_n
            m = jnp.maximum(jnp.max(sp, axis=-1, keepdims=True), jnp.max(sn, axis=-1, keepdims=True))
            ep = jnp.exp2(sp - m)
            en = jnp.exp2(sn - m)
            inv = 1.0 / (jnp.sum(ep, axis=-1, keepdims=True) + jnp.sum(en, axis=-1, keepdims=True))
            if coef is None:
                p_p, p_n = ep * inv, en * inv
            else:
                p_p, p_n = p_p - coef * (ep * inv), p_n - coef * (en * inv)
        o = _dot(p_p.astype(BF16), _cache_head(cv_ref, hh)) + _dot(p_n.astype(BF16), vn_ref[:, sl])
        outs.append(_rms_rows(o, sg_ref[...]) * (1.0 - lam_init))
    o_ref[...] = jnp.concatenate(outs, axis=-1).astype(BF16)


def _diff_sample(q, kn, vn, ck, cv, slopes, lam, subln_g, lam_init, layer):
    nb, t, _ = q.shape
    past = ck.shape[2] // N_HEADS
    new, cache = _sample_specs(t, past, layer)
    return pl.pallas_call(
        functools.partial(_diff_sample_kernel, past=past, t=t, lam_init=lam_init),
        grid=(nb,),
        in_specs=[new, new, new, cache, cache,
                  pl.BlockSpec((N_HEADS, 1, HEAD_DIM), lambda b: (0, 0, 0)),
                  pl.BlockSpec((4, DIFF_QK), lambda b: (0, 0)),
                  pl.BlockSpec((1, HEAD_DIM), lambda b: (0, 0))],
        out_specs=new,
        out_shape=jax.ShapeDtypeStruct((nb, t, GROUP_W), BF16),
        compiler_params=_cparams("arbitrary"),
        name="diff_sample",
    )(q, kn, vn, ck, cv, slopes, lam, subln_g)


def _outproj_kernel(x_ref, osb_ref, ofx_ref, odf_ref, w_ref, o_ref):
    y = _dot(osb_ref[...], w_ref[0:GROUP_W, :])
    y += _dot(ofx_ref[...], w_ref[GROUP_W:2 * GROUP_W, :])
    y += _dot(odf_ref[...], w_ref[2 * GROUP_W:3 * GROUP_W, :])
    o_ref[...] = x_ref[...] + y


def _outproj(x, tile0, o_sb, o_fox, o_diff, w, tm):
    rows = o_sb.shape[0]
    d = x.shape[1]
    row = lambda i: (i, 0)
    xrow = lambda i: (i + tile0, 0)
    slab = pl.BlockSpec((tm, GROUP_W), row)
    return pl.pallas_call(
        _outproj_kernel,
        grid=(rows // tm,),
        in_specs=[pl.BlockSpec((tm, d), xrow), slab, slab, slab, _vmem_whole()],
        out_specs=pl.BlockSpec((tm, d), xrow),
        out_shape=jax.ShapeDtypeStruct(x.shape, F32),
        input_output_aliases={0: 0},
        compiler_params=_cparams("arbitrary"),
        name="outproj",
    )(x, o_sb, o_fox, o_diff, w)


def _swiglu_part(h, wg_ref, wu_ref, wd_ref):
    gate = _dot(h, wg_ref[...])
    up = _dot(h, wu_ref[...])
    a = gate * jax.nn.sigmoid(gate) * up
    return _dot(a.astype(BF16), wd_ref[...])


def _ffn_kernel(x_ref, g_ref, wg_ref, wu_ref, wd_ref, o_ref, h_ref):
    f = pl.program_id(1)

    @pl.when(f == 0)
    def _():
        h_ref[...] = _rms_rows(x_ref[...], g_ref[...]).astype(BF16)

    part = _swiglu_part(h_ref[...], wg_ref, wu_ref, wd_ref)

    @pl.when(f == 0)
    def _():
        o_ref[...] = x_ref[...] + part

    @pl.when(f != 0)
    def _():
        o_ref[...] += part


def _ffn_dense(x, g, wg, wu, wd, tm, tf):
    t, d = x.shape
    nf = wg.shape[1] // tf
    return pl.pallas_call(
        _ffn_kernel,
        grid=(t // tm, nf),
        in_specs=[pl.BlockSpec((tm, d), lambda i, f: (i, 0)),
                  pl.BlockSpec((1, d), lambda i, f: (0, 0)),
                  pl.BlockSpec((d, tf), lambda i, f: (0, f)),
                  pl.BlockSpec((d, tf), lambda i, f: (0, f)),
                  pl.BlockSpec((tf, d), lambda i, f: (f, 0))],
        out_specs=pl.BlockSpec((tm, d), lambda i, f: (i, 0)),
        out_shape=jax.ShapeDtypeStruct((t, d), F32),
        scratch_shapes=[pltpu.VMEM((tm, d), BF16)],
        compiler_params=_cparams("arbitrary", "arbitrary"),
        name="ffn_dense",
    )(x, g, wg, wu, wd)


def _route_kernel(x_ref, g_ref, wr_ref, h_ref, gates_ref, idx_ref):
    hf = _rms_rows(x_ref[...], g_ref[...])
    h_ref[...] = hf.astype(BF16)
    logits = _dot_f32(hf, wr_ref[...])
    lane = lax.broadcasted_iota(jnp.int32, logits.shape, 1)
    m1 = jnp.max(logits, axis=-1, keepdims=True)
    i1 = jnp.min(jnp.where(logits == m1, lane, N_EXPERTS), axis=-1, keepdims=True)
    rest = jnp.where(lane == i1, NEG_INF, logits)
    m2 = jnp.max(rest, axis=-1, keepdims=True)
    i2 = jnp.min(jnp.where(rest == m2, lane, N_EXPERTS), axis=-1, keepdims=True)
    e2 = jnp.exp(m2 - m1)
    inv = 1.0 / (1.0 + e2)
    gates_ref[...] = jnp.where(lane == 0, inv, jnp.where(lane == 1, e2 * inv, 0.0))
    idx_ref[...] = jnp.where(lane == 0, i1, jnp.where(lane == 1, i2, 0))


def _moe_route(x, g, wr, tm):
    t, d = x.shape
    ne = wr.shape[1]
    row = lambda i: (i, 0)
    return pl.pallas_call(
        _route_kernel,
        grid=(t // tm,),
        in_specs=[pl.BlockSpec((tm, d), row), pl.BlockSpec((1, d), lambda i: (0, 0)),
                  pl.BlockSpec((d, ne), lambda i: (0, 0))],
        out_specs=[pl.BlockSpec((tm, d), row), pl.BlockSpec((tm, ne), row), pl.BlockSpec((tm, ne), row)],
        out_shape=[jax.ShapeDtypeStruct((t, d), BF16), jax.ShapeDtypeStruct((t, ne), F32),
                   jax.ShapeDtypeStruct((t, ne), jnp.int32)],
        compiler_params=_cparams("arbitrary"),
        name="moe_route",
    )(x, g, wr)


def _expert_kernel(vt_ref, ve_ref, vf_ref, st_ref, en_ref, xs_ref, wg_ref, wu_ref, wd_ref, o_ref, h_ref, *, tm):
    v = pl.program_id(0)
    f = pl.program_id(1)
    lo = jnp.maximum(st_ref[v], vt_ref[v] * tm)
    hi = jnp.minimum(en_ref[v], (vt_ref[v] + 1) * tm)

    @pl.when(hi > lo)
    def _():
        @pl.when(f == 0)
        def _():
            rows = lax.broadcasted_iota(jnp.int32, (tm, 1), 0) + vt_ref[v] * tm
            mine = (rows >= lo) & (rows < hi)
            h_ref[...] = jnp.where(mine, xs_ref[...], jnp.zeros_like(xs_ref))

        part = _swiglu_part(h_ref[...], wg_ref, wu_ref, wd_ref)
        fresh = (vf_ref[v] == 1) & (f == 0)

        @pl.when(fresh)
        def _():
            o_ref[...] = part

        @pl.when(jnp.logical_not(fresh))
        def _():
            o_ref[...] += part


def _moe_experts(xs, sched, wg, wu, wd, tm, tf):
    r, d = xs.shape
    ne, _, fe = wg.shape
    tpe = fe // tf
    n_visits = sched[0].shape[0]
    grid_spec = pltpu.PrefetchScalarGridSpec(
        num_scalar_prefetch=5,
        grid=(n_visits, tpe),
        in_specs=[pl.BlockSpec((tm, d), lambda v, f, vt, ve, vf, st, en: (vt[v], 0)),
                  pl.BlockSpec((None, d, tf), lambda v, f, vt, ve, vf, st, en: (ve[v], 0, f)),
                  pl.BlockSpec((None, d, tf), lambda v, f, vt, ve, vf, st, en: (ve[v], 0, f)),
                  pl.BlockSpec((None, tf, d), lambda v, f, vt, ve, vf, st, en: (ve[v], f, 0))],
        out_specs=pl.BlockSpec((tm, d), lambda v, f, vt, ve, vf, st, en: (vt[v], 0)),
        scratch_shapes=[pltpu.VMEM((tm, d), BF16)])
    return pl.pallas_call(
        functools.partial(_expert_kernel, tm=tm),
        grid_spec=grid_spec,
        out_shape=jax.ShapeDtypeStruct((r, d), F32),
        compiler_params=_cparams("arbitrary", "arbitrary"),
        name="moe_experts",
    )(*sched, xs, wg, wu, wd)


def _combine_kernel(x_ref, y1_ref, y2_ref, gates_ref, o_ref):
    gates = gates_ref[...]
    o_ref[...] = x_ref[...] + (gates[:, 0:1] * y1_ref[...] + gates[:, 1:2] * y2_ref[...])


def _moe_combine(x, y1, y2, gates, tm):
    t, d = x.shape
    row = lambda i: (i, 0)
    wide = pl.BlockSpec((tm, d), row)
    return pl.pallas_call(
        _combine_kernel,
        grid=(t // tm,),
        in_specs=[wide, wide, wide, pl.BlockSpec((tm, gates.shape[1]), row)],
        out_specs=wide,
        out_shape=jax.ShapeDtypeStruct((t, d), F32),
        compiler_params=_cparams("arbitrary"),
        name="moe_combine",
    )(x, y1, y2, gates)


def _expert_schedule(idx, tm):
    t = idx.shape[0]
    i12 = idx[:, :2]
    order = jnp.argsort(i12.reshape(-1), stable=True).astype(jnp.int32)
    src = order // 2
    onehot = (i12[:, :, None] == jnp.arange(N_EXPERTS, dtype=jnp.int32)).sum(axis=1).astype(jnp.int32)
    csum = jnp.cumsum(onehot, axis=0)
    counts = csum[-1]
    ends = jnp.cumsum(counts)
    starts = ends - counts
    pos = jnp.take_along_axis(starts[None, :] + csum - onehot, i12, axis=1)

    n_tiles = (2 * t) // tm
    n_visits = n_tiles + N_EXPERTS - 1
    first_tile = starts // tm
    nvis = jnp.where(counts > 0, (ends - 1) // tm - first_tile + 1, 0)
    vend = jnp.cumsum(nvis)
    v = jnp.arange(n_visits, dtype=jnp.int32)
    active = v < vend[-1]
    ve = jnp.minimum(jnp.sum(v[:, None] >= vend[None, :], axis=1), N_EXPERTS - 1).astype(jnp.int32)
    vt = jnp.where(active, first_tile[ve] + v - (vend - nvis)[ve], n_tiles - 1).astype(jnp.int32)
    vf = jnp.concatenate([jnp.ones((1,), jnp.int32), (vt[1:] != vt[:-1]).astype(jnp.int32)])
    st = jnp.where(active, starts[ve], 0).astype(jnp.int32)
    en = jnp.where(active, ends[ve], 0).astype(jnp.int32)
    return src, pos, (vt, ve, vf, st, en)


def _ffn_moe(x, g, wg, wu, wd, wr, tm, tf):
    t = x.shape[0]
    tme = _pick_tile(2 * t, (512, 256, 128, 64, 8))
    h, gates, idx = _moe_route(x, g, wr, tm)
    src, pos, sched = _expert_schedule(idx, tme)
    ys = _moe_experts(jnp.take(h, src, axis=0), sched, wg, wu, wd, tme, tf)
    return _moe_combine(x, jnp.take(ys, pos[:, 0], axis=0), jnp.take(ys, pos[:, 1], axis=0), gates, tm)


def _pick_tile(n, prefs):
    for p in prefs:
        if n % p == 0:
            return p
    return n


def _round_up(n, m):
    return (n + m - 1) // m * m


def kernel(x_prompt, x_sample, cache_sb_k, cache_sb_v, cache_fox_k, cache_fox_v, cache_fox_logf, cache_diff_k, cache_diff_v, norm_mix_g, norm_ffn_g, w_in, b_fox_f, fox_q_g, fox_k_g, diff_q_g, diff_k_g, diff_lam, diff_subln_g, w_out, w_ffn_gate, w_ffn_up, w_ffn_down, w_router, w_moe_gate, w_moe_up, w_moe_down):
    bp, tp, d = x_prompt.shape
    nb, ts, _ = x_sample.shape
    depth = w_in.shape[0]
    past = cache_sb_k.shape[2]
    assert bp == 1
    n_s = nb * ts
    t_all = tp + n_s
    tm_in = math.gcd(256, math.gcd(tp, n_s))
    tm_out = math.gcd(512, math.gcd(tp, n_s))
    tm = _pick_tile(t_all, (512, 256, 128, 64, 8))
    tq_sb, tk_sb = _pick_tile(tp, SB_TILE[:1] + (256, 128)), _pick_tile(tp, SB_TILE[1:] + (128,))
    tq_fx, tk_fx = _pick_tile(tp, FOX_TILE[:1] + (256, 128)), _pick_tile(tp, FOX_TILE[1:] + (256, 128))
    tq_df, tk_df = _pick_tile(tp, DIFF_TILE[:1] + (256, 128)), _pick_tile(tp, DIFF_TILE[1:] + (256, 128))

    w3 = 3 * GROUP_W
    w_in_r = jnp.concatenate(
        [w_in[:, :, :2 * w3], w_in[:, :, 2 * w3 + N_HEADS:], w_in[:, :, 2 * w3:2 * w3 + N_HEADS],
         jnp.zeros((depth, d, HEAD_DIM - N_HEADS), w_in.dtype)], axis=-1).astype(BF16)
    b_f = jnp.pad(b_fox_f, ((0, 0), (0, HEAD_DIM - N_HEADS)))
    w_out_b = w_out.astype(BF16)
    w_fg, w_fu, w_fd = w_ffn_gate.astype(BF16), w_ffn_up.astype(BF16), w_ffn_down.astype(BF16)
    ff = w_ffn_gate.shape[-1]
    tf = _pick_tile(ff, (512, 256, 128))
    ffe = w_moe_gate.shape[-1]
    tfe = 512 if ffe >= 512 else 128
    pad_e = _round_up(ffe, tfe) - ffe
    w_mg = jnp.pad(w_moe_gate, ((0, 0), (0, 0), (0, 0), (0, pad_e))).astype(BF16)
    w_mu = jnp.pad(w_moe_up, ((0, 0), (0, 0), (0, 0), (0, pad_e))).astype(BF16)
    w_md = jnp.pad(w_moe_down, ((0, 0), (0, 0), (0, pad_e), (0, 0))).astype(BF16)

    slopes = jnp.exp2(-8.0 * jnp.arange(1, N_HEADS + 1, dtype=F32) / N_HEADS)
    slopes = jnp.broadcast_to(slopes[:, None, None], (N_HEADS, 1, HEAD_DIM))

    c_sbk, c_sbv, c_fxk, c_fxv, c_dfk, c_dfv = [
        c.reshape(depth, nb, past * N_HEADS, HEAD_DIM)
        for c in (cache_sb_k, cache_sb_v, cache_fox_k, cache_fox_v, cache_diff_k, cache_diff_v)]

    x = jnp.concatenate([x_prompt.reshape(tp, d), x_sample.reshape(n_s, d)], axis=0)
    stacked_p, stacked_s = (), ()
    logf_p, logf_s = [], []
    for l in range(depth):
        lam_init = 0.8 - 0.6 * math.exp(-0.3 * l)
        dqg = jnp.tile(diff_q_g[l], 2)[None, :]
        dkg = jnp.tile(diff_k_g[l], 2)[None, :]
        proj_args = (norm_mix_g[l][None, :], w_in_r[l], b_f[l][None, :],
                     fox_q_g[l][None, :], fox_k_g[l][None, :], dqg, dkg, tm_in)
        sg = diff_subln_g[l][None, :]
        lam = diff_lam[l]

        res = _inproj(x, 0, tp, l, depth, stacked_p, *proj_args, True)
        sbq, sbk, sbv, fxq, fxk, fxv, dfq, dfk, dfv = res[:9]
        stacked_p = tuple(res[9:9 + N_STACKED])
        logf_p.append(res[15][:, :N_HEADS])
        cum_k = res[16][:, :N_HEADS].T.reshape(N_HEADS, tp // tk_fx, 1, tk_fx)
        o_sb = _sb_prompt(sbq, sbk, sbv, tp, tq_sb, tk_sb)
        o_fx = _fox_prompt(fxq, fxk, fxv, cum_k, tp, tq_fx, tk_fx)
        o_df = _diff_prompt(dfq, dfk, dfv, slopes, lam, sg, tp, tq_df, tk_df, lam_init)
        x = _outproj(x, 0, o_sb, o_fx, o_df, w_out_b[l], tm_out)

        res = _inproj(x, tp // tm_in, n_s, l, depth, stacked_s, *proj_args, False)
        sbq, sbk, sbv, fxq, fxk, fxv, dfq, dfk, dfv = [a.reshape(nb, ts, GROUP_W) for a in res[:9]]
        stacked_s = tuple(res[9:9 + N_STACKED])
        lf_s = res[15][:, :N_HEADS]
        logf_s.append(lf_s)
        lf_new = jnp.pad(jnp.swapaxes(lf_s.reshape(nb, ts, N_HEADS), 1, 2), ((0, 0), (0, 8 - N_HEADS), (0, 0)))
        lf_past = jnp.pad(jnp.swapaxes(cache_fox_logf[l], 1, 2), ((0, 0), (0, 8 - N_HEADS), (0, 0)))
        o_sb = _sb_sample(sbq, sbk, sbv, c_sbk, c_sbv, l)
        o_fx = _fox_sample(fxq, fxk, fxv, c_fxk, c_fxv, lf_past, lf_new, l)
        o_df = _diff_sample(dfq, dfk, dfv, c_dfk, c_dfv, slopes, lam, sg, lam_init, l)
        x = _outproj(x, tp // tm_out, o_sb.reshape(n_s, GROUP_W), o_fx.reshape(n_s, GROUP_W),
                     o_df.reshape(n_s, GROUP_W), w_out_b[l], tm_out)

        if l % 2 == 0:
            x = _ffn_dense(x, norm_ffn_g[l][None, :], w_fg[l // 2], w_fu[l // 2], w_fd[l // 2], tm, tf)
        else:
            x = _ffn_moe(x, norm_ffn_g[l][None, :], w_mg[l // 2], w_mu[l // 2], w_md[l // 2],
                         w_router[l // 2], tm, tfe)

    kv_tail = (N_HEADS, HEAD_DIM)
    sbk_p, sbv_p, fxk_p, fxv_p, dfk_p, dfv_p = [a.reshape((depth, 1, tp) + kv_tail) for a in stacked_p]
    sbk_s, sbv_s, fxk_s, fxv_s, dfk_s, dfv_s = [a.reshape((depth, nb, ts) + kv_tail) for a in stacked_s]
    lf_p = jnp.stack(logf_p).reshape(depth, 1, tp, N_HEADS)
    lf_s = jnp.stack(logf_s).reshape(depth, nb, ts, N_HEADS)
    return (x[:tp].reshape(1, tp, d), x[tp:].reshape(nb, ts, d),
            sbk_p, sbv_p, fxk_p, fxv_p, lf_p, dfk_p, dfv_p,
            sbk_s, sbv_s, fxk_s, fxv_s, lf_s, dfk_s, dfv_s)
```

```python
import functools
import math

import jax
import jax.numpy as jnp
from jax import lax
from jax.experimental import pallas as pl
from jax.experimental.pallas import tpu as pltpu

F32 = jnp.float32
BF16 = jnp.bfloat16

LANES = 128
HEAD_DIM = 128
N_HEADS = 4
GROUP_W = N_HEADS * HEAD_DIM
DIFF_QK = HEAD_DIM // 2
CHUNK = 64
CHUNK_SHIFT = 6
N_EXPERTS = 8
RMS_EPS = 1e-6
LOG2E = math.log2(math.e)
VMEM_LIMIT = 56 * 1024 * 1024
SB_TILE = (2048, 256)
FOX_TILE = (1024, 1024)
DIFF_TILE = (1024, 1024)
NEG_INF = float("-inf")
SIGN_BIT = -2 ** 31
HIGHEST = lax.Precision.HIGHEST


def _cparams(*sem):
    return pltpu.CompilerParams(dimension_semantics=sem, vmem_limit_bytes=VMEM_LIMIT)


def _vmem_whole():
    return pl.BlockSpec(memory_space=pltpu.VMEM)


def _any_space():
    return pl.BlockSpec(memory_space=pl.ANY)


def _rms_rows(x, g):
    return x * lax.rsqrt(jnp.mean(x * x, axis=-1, keepdims=True) + RMS_EPS) * g


def _log_sigmoid(x):
    return -(jnp.maximum(-x, 0.0) + jnp.log1p(jnp.exp(-jnp.abs(x))))


def _softplus2(z):
    neg_abs = lax.bitcast_convert_type(lax.bitcast_convert_type(z, jnp.int32) | SIGN_BIT, F32)
    return jnp.maximum(z, 0.0) + jnp.log2(1.0 + jnp.exp2(neg_abs))


def _dot(a, b):
    return jnp.dot(a, b, preferred_element_type=F32)


def _dot_f32(a, b):
    return jnp.dot(a, b, preferred_element_type=F32, precision=HIGHEST)


def _dot_nt(a, b):
    return lax.dot_general(a, b, (((1,), (1,)), ((), ())), preferred_element_type=F32)


def _lane_blocks(x):
    return [x[:, c * LANES:(c + 1) * LANES] for c in range(x.shape[1] // LANES)]


def _sum_list(xs):
    acc = xs[0]
    for x in xs[1:]:
        acc = acc + x
    return acc


def _max_list(xs):
    acc = xs[0]
    for x in xs[1:]:
        acc = jnp.maximum(acc, x)
    return acc


def _head_rms(z, g, group):
    outs = []
    for hh in range(N_HEADS):
        zz = z[:, hh * HEAD_DIM:(hh + 1) * HEAD_DIM]
        sq = zz * zz
        if group == HEAD_DIM:
            r = lax.rsqrt(jnp.mean(sq, axis=-1, keepdims=True) + RMS_EPS)
        else:
            lo = lax.broadcasted_iota(jnp.int32, zz.shape, 1) < DIFF_QK
            s_lo = jnp.sum(jnp.where(lo, sq, 0.0), axis=-1, keepdims=True)
            s_hi = jnp.sum(jnp.where(lo, 0.0, sq), axis=-1, keepdims=True)
            r = jnp.where(lo, lax.rsqrt(s_lo / DIFF_QK + RMS_EPS), lax.rsqrt(s_hi / DIFF_QK + RMS_EPS))
        outs.append(zz * r * g)
    return jnp.concatenate(outs, axis=-1)


N_STACKED = 6


def _store_heads(o_ref, z):
    rows = z.shape[0]
    for hh in range(N_HEADS):
        o_ref[pl.ds(hh, rows, stride=N_HEADS), :] = z[:, hh * HEAD_DIM:(hh + 1) * HEAD_DIM]


def _inproj_kernel(*refs, with_cumsum, aliased):
    x_ref, g_ref, w_ref, bf_ref, fqg_ref, fkg_ref, dqg_ref, dkg_ref, tri_ref = refs[:9]
    outs = refs[9 + (N_STACKED if aliased else 0):]
    (sbq, sbk, sbv, fxq, fxk, fxv, dfq, dfk, dfv,
     sbk32, sbv32, fxk32, fxv32, dfk32, dfv32, logf, cumf) = outs[:17]
    h = _rms_rows(x_ref[...], g_ref[...]).astype(BF16)

    def grp(c):
        return _dot(h, w_ref[:, c * GROUP_W:(c + 1) * GROUP_W])

    z = grp(0)
    sbq[...] = (z * (HEAD_DIM ** -0.5 * LOG2E)).astype(BF16)
    z = grp(1)
    _store_heads(sbk32, z)
    sbk[...] = z.astype(BF16)
    z = grp(2)
    _store_heads(sbv32, z)
    sbv[...] = z.astype(BF16)

    z = _head_rms(grp(3), fqg_ref[...], HEAD_DIM)
    fxq[...] = (z * (HEAD_DIM ** -0.5 * LOG2E)).astype(BF16)
    z = _head_rms(grp(4), fkg_ref[...], HEAD_DIM)
    _store_heads(fxk32, z)
    fxk[...] = z.astype(BF16)
    z = grp(5)
    _store_heads(fxv32, z)
    fxv[...] = z.astype(BF16)

    z = _head_rms(grp(6), dqg_ref[...], DIFF_QK)
    dfq[...] = (z * (DIFF_QK ** -0.5 * LOG2E)).astype(BF16)
    z = _head_rms(grp(7), dkg_ref[...], DIFF_QK)
    _store_heads(dfk32, z)
    dfk[...] = z.astype(BF16)
    z = grp(8)
    _store_heads(dfv32, z)
    dfv[...] = z.astype(BF16)

    zf = _dot(h, w_ref[:, 9 * GROUP_W:9 * GROUP_W + HEAD_DIM])
    lf = _log_sigmoid(zf + bf_ref[...])
    logf[...] = lf
    if with_cumsum:
        carry_ref = outs[17]

        @pl.when(pl.program_id(0) == 0)
        def _():
            carry_ref[...] = jnp.zeros_like(carry_ref)

        cum = _dot_f32(tri_ref[...], lf) + carry_ref[0:1, :]
        cumf[...] = cum
        tm = cum.shape[0]
        carry_ref[...] = jnp.broadcast_to(cum[tm - 1:tm, :], carry_ref.shape)
    else:
        cumf[...] = lf


def _inproj(x, tile0, rows, layer, depth, stacked, g, w, bf, fqg, fkg, dqg, dkg, tm, with_cumsum):
    d = x.shape[1]
    aliased = layer > 0
    row = lambda i: (i, 0)
    full = lambda i: (0, 0)
    small = pl.BlockSpec((1, HEAD_DIM), full)
    slab = pl.BlockSpec((tm, GROUP_W), row)
    narrow = pl.BlockSpec((tm, HEAD_DIM), row)
    stacked_spec = pl.BlockSpec((None, tm * N_HEADS, HEAD_DIM), lambda i: (layer, i, 0))
    tri = jnp.tri(tm, dtype=F32)
    in_specs = [pl.BlockSpec((tm, d), lambda i: (i + tile0, 0)), pl.BlockSpec((1, d), full), _vmem_whole(),
                small, small, small, small, small, pl.BlockSpec((tm, tm), full)]
    args = [x, g, w, bf, fqg, fkg, dqg, dkg, tri]
    aliases = {}
    if aliased:
        in_specs += [_any_space()] * N_STACKED
        aliases = {len(args) + k: 9 + k for k in range(N_STACKED)}
        args += list(stacked)
    out_bf = jax.ShapeDtypeStruct((rows, GROUP_W), BF16)
    out_n = jax.ShapeDtypeStruct((rows, HEAD_DIM), F32)
    out_st = jax.ShapeDtypeStruct((depth, rows * N_HEADS, HEAD_DIM), F32)
    return pl.pallas_call(
        functools.partial(_inproj_kernel, with_cumsum=with_cumsum, aliased=aliased),
        grid=(rows // tm,),
        in_specs=in_specs,
        out_specs=[slab] * 9 + [stacked_spec] * N_STACKED + [narrow, narrow],
        out_shape=[out_bf] * 9 + [out_st] * N_STACKED + [out_n, out_n],
        scratch_shapes=[pltpu.VMEM((8, HEAD_DIM), F32)] if with_cumsum else [],
        input_output_aliases=aliases,
        compiler_params=_cparams("arbitrary"),
        name="inproj",
    )(*args)


def _sb_prompt_kernel(q_ref, k_ref, v_ref, tri_ref, o_ref, c_ref, acc_ref, *, tq, tk):
    i = pl.program_id(1)
    ratio = tq // tk
    row = lax.broadcasted_iota(jnp.int32, (tk, tk), 0)
    col = lax.broadcasted_iota(jnp.int32, (tk, tk), 1)
    strictly_before = col < row
    c_ref[...] = jnp.zeros_like(c_ref)
    acc_ref[...] = jnp.zeros_like(acc_ref)

    def step(j, r0, r1, masked):
        kc, vc = _kv_chunk(k_ref, v_ref, j, tk)
        z = _dot_nt(q_ref[r0:r1, :], kc)
        cost = _softplus2(z)
        if masked:
            cost = jnp.where(strictly_before, cost, 0.0)
        incl = _dot(cost.astype(BF16), tri_ref[...])
        w = jnp.exp2(z - incl)
        if masked:
            w = jnp.where(strictly_before, w, 0.0)
        carried = c_ref[r0:r1, :]
        acc_ref[r0:r1, :] += jnp.exp2(-carried) * _dot(w.astype(BF16), vc)
        c_ref[r0:r1, :] = carried + incl[:, 0:1]

    for d in reversed(range(ratio)):
        step(i * ratio + d, d * tk, (d + 1) * tk, True)
        if d + 1 < ratio:
            step(i * ratio + d, (d + 1) * tk, tq, False)

    def body(jj, carry):
        step(i * ratio - 1 - jj, 0, tq, False)
        return carry

    lax.fori_loop(0, i * ratio, body, 0)
    o_ref[...] = acc_ref[...].astype(BF16)


def _sb_prompt(q, k, v, tp, tq, tk):
    tri = jnp.tri(tk, dtype=BF16)
    return pl.pallas_call(
        functools.partial(_sb_prompt_kernel, tq=tq, tk=tk),
        grid=(N_HEADS, tp // tq),
        in_specs=[pl.BlockSpec((tq, HEAD_DIM), lambda h, i: (i, h)),
                  _kv_spec(tp), _kv_spec(tp),
                  pl.BlockSpec((tk, tk), lambda h, i: (0, 0))],
        out_specs=pl.BlockSpec((tq, HEAD_DIM), lambda h, i: (i, h)),
        out_shape=jax.ShapeDtypeStruct((tp, GROUP_W), BF16),
        scratch_shapes=[pltpu.VMEM((tq, LANES), F32), pltpu.VMEM((tq, HEAD_DIM), F32)],
        compiler_params=_cparams("arbitrary", "arbitrary"),
        name="sb_prompt",
    )(q, k, v, tri)


def _softmax_step(s, vc, m_ref, l_ref, acc_ref):
    blocks = _lane_blocks(s)
    m_prev = m_ref[...]
    m_new = jnp.maximum(m_prev, jnp.max(_max_list(blocks), axis=-1, keepdims=True))
    alpha = jnp.exp2(m_prev - m_new)
    ps = [jnp.exp2(b - m_new) for b in blocks]
    l_ref[...] = alpha * l_ref[...] + _sum_list(ps)
    acc_ref[...] = alpha * acc_ref[...] + _dot(jnp.concatenate(ps, axis=1).astype(BF16), vc)
    m_ref[...] = m_new


def _softmax_init(m_ref, l_ref, acc_ref):
    m_ref[...] = jnp.full_like(m_ref, NEG_INF)
    l_ref[...] = jnp.zeros_like(l_ref)
    acc_ref[...] = jnp.zeros_like(acc_ref)


def _softmax_result(l_ref, acc_ref):
    return acc_ref[...] * (1.0 / jnp.sum(l_ref[...], axis=-1, keepdims=True))


def _diag_chunks(i, tq, tk):
    if tk >= tq:
        n_full = (i * tq) // tk
        return n_full, [n_full]
    ratio = tq // tk
    return i * ratio, [i * ratio + d for d in range(ratio)]


def _kv_chunk(k_ref, v_ref, j, tk):
    start = pl.multiple_of(j * tk, tk)
    return k_ref[pl.ds(start, tk), :], v_ref[pl.ds(start, tk), :]


def _fox_prompt_kernel(q_ref, k_ref, v_ref, fk_ref, o_ref, m_ref, l_ref, acc_ref, *, tq, tk):
    i = pl.program_id(1)
    q = q_ref[...]
    n_full, diag = _diag_chunks(i, tq, tk)
    f0 = fk_ref[diag[0]][:, 0:1]
    _softmax_init(m_ref, l_ref, acc_ref)

    def step(j, masked):
        kc, vc = _kv_chunk(k_ref, v_ref, j, tk)
        s = _dot_nt(q, kc) + (f0 - fk_ref[j]) * LOG2E
        if masked:
            row = lax.broadcasted_iota(jnp.int32, (tq, tk), 0) + i * tq
            col = lax.broadcasted_iota(jnp.int32, (tq, tk), 1) + j * tk
            s = jnp.where(col <= row, s, NEG_INF)
        _softmax_step(s, vc, m_ref, l_ref, acc_ref)

    for j in diag:
        step(j, True)

    def body(j, carry):
        step(j, False)
        return carry

    lax.fori_loop(0, n_full, body, 0)
    o_ref[...] = _softmax_result(l_ref, acc_ref).astype(BF16)


def _kv_spec(tp):
    return pl.BlockSpec((tp, HEAD_DIM), lambda h, i: (0, h), pipeline_mode=pl.Buffered(1))


def _fox_prompt(q, k, v, fk, tp, tq, tk):
    stat = lambda: pltpu.VMEM((tq, LANES), F32)
    return pl.pallas_call(
        functools.partial(_fox_prompt_kernel, tq=tq, tk=tk),
        grid=(N_HEADS, tp // tq),
        in_specs=[pl.BlockSpec((tq, HEAD_DIM), lambda h, i: (i, h)),
                  _kv_spec(tp), _kv_spec(tp),
                  pl.BlockSpec((None, tp // tk, 1, tk), lambda h, i: (h, 0, 0, 0))],
        out_specs=pl.BlockSpec((tq, HEAD_DIM), lambda h, i: (i, h)),
        out_shape=jax.ShapeDtypeStruct((tp, GROUP_W), BF16),
        scratch_shapes=[stat(), stat(), pltpu.VMEM((tq, HEAD_DIM), F32)],
        compiler_params=_cparams("arbitrary", "arbitrary"),
        name="fox_prompt",
    )(q, k, v, fk)


def _diff_lambda(lam_ref, lam_init):
    lv = lam_ref[...]
    a = jnp.sum(lv[0:1, :] * lv[1:2, :], axis=-1, keepdims=True)
    b = jnp.sum(lv[2:3, :] * lv[3:4, :], axis=-1, keepdims=True)
    return jnp.exp(a) - jnp.exp(b) + lam_init


def _split_halves(q):
    first = lax.broadcasted_iota(jnp.int32, q.shape, 1) < DIFF_QK
    zero = jnp.zeros_like(q)
    return jnp.where(first, q, zero), jnp.where(first, zero, q)


def _diff_prompt_kernel(q_ref, k_ref, v_ref, slope_ref, lam_ref, sg_ref, o_ref,
                        m1_ref, l1_ref, a1_ref, m2_ref, l2_ref, a2_ref, *, tq, tk, lam_init):
    i = pl.program_id(1)
    q1, q2 = _split_halves(q_ref[...])
    slope = slope_ref[:, 0:1] * LOG2E
    n_full, diag = _diag_chunks(i, tq, tk)
    _softmax_init(m1_ref, l1_ref, a1_ref)
    _softmax_init(m2_ref, l2_ref, a2_ref)

    def step(j, bias):
        kc, vc = _kv_chunk(k_ref, v_ref, j, tk)
        _softmax_step(_dot_nt(q1, kc) + bias, vc, m1_ref, l1_ref, a1_ref)
        _softmax_step(_dot_nt(q2, kc) + bias, vc, m2_ref, l2_ref, a2_ref)

    for j in diag:
        t = lax.broadcasted_iota(jnp.int32, (tq, tk), 0) + i * tq
        s = lax.broadcasted_iota(jnp.int32, (tq, tk), 1) + j * tk
        visible = lax.shift_right_logical(s, CHUNK_SHIFT) <= lax.shift_right_logical(t, CHUNK_SHIFT)
        rel = (t - i * tq - jnp.abs(t - s)).astype(F32)
        step(j, jnp.where(visible, slope * rel, NEG_INF))

    col_row = lax.broadcasted_iota(jnp.int32, (1, tk), 1)

    def body(j, carry):
        step(j, slope * (col_row + (j * tk - i * tq)).astype(F32))
        return carry

    lax.fori_loop(0, n_full, body, 0)
    lam = _diff_lambda(lam_ref, lam_init)
    o = _softmax_result(l1_ref, a1_ref) - lam * _softmax_result(l2_ref, a2_ref)
    o_ref[...] = (_rms_rows(o, sg_ref[...]) * (1.0 - lam_init)).astype(BF16)


def _diff_prompt(q, k, v, slopes, lam, subln_g, tp, tq, tk, lam_init):
    stat = lambda: pltpu.VMEM((tq, LANES), F32)
    acc = lambda: pltpu.VMEM((tq, HEAD_DIM), F32)
    return pl.pallas_call(
        functools.partial(_diff_prompt_kernel, tq=tq, tk=tk, lam_init=lam_init),
        grid=(N_HEADS, tp // tq),
        in_specs=[pl.BlockSpec((tq, HEAD_DIM), lambda h, i: (i, h)),
                  _kv_spec(tp), _kv_spec(tp),
                  pl.BlockSpec((None, 1, HEAD_DIM), lambda h, i: (h, 0, 0)),
                  pl.BlockSpec((4, DIFF_QK), lambda h, i: (0, 0)),
                  pl.BlockSpec((1, HEAD_DIM), lambda h, i: (0, 0))],
        out_specs=pl.BlockSpec((tq, HEAD_DIM), lambda h, i: (i, h)),
        out_shape=jax.ShapeDtypeStruct((tp, GROUP_W), BF16),
        scratch_shapes=[stat(), stat(), acc(), stat(), stat(), acc()],
        compiler_params=_cparams("arbitrary", "arbitrary"),
        name="diff_prompt",
    )(q, k, v, slopes, lam, subln_g)


def _split_bf16(x):
    hi = x.astype(BF16)
    return hi, (x - hi.astype(F32)).astype(BF16)


def _sb_sample_kernel(q_ref, kn_ref, vn_ref, ck_ref, cv_ref, trip_ref, trin_ref, o_ref, *, past, t):
    rown = lax.broadcasted_iota(jnp.int32, (t, t), 0)
    coln = lax.broadcasted_iota(jnp.int32, (t, t), 1)
    allowed_n = coln < rown
    outs = []
    for hh in range(N_HEADS):
        sl = slice(hh * HEAD_DIM, (hh + 1) * HEAD_DIM)
        q = q_ref[:, sl]
        kp = _cache_head(ck_ref, hh)
        vp = _cache_head(cv_ref, hh)
        zp = _dot_nt(q, kp)
        zn = _dot_nt(q, kn_ref[:, sl])
        cp = _softplus2(zp)
        cn = jnp.where(allowed_n, _softplus2(zn), 0.0)
        cn_hi, cn_lo = _split_bf16(cn)
        cp_hi, cp_lo = _split_bf16(cp)
        later_n = _dot(cn_hi, trin_ref[...]) + _dot(cn_lo, trin_ref[...])
        later_p = (_dot(cp_hi, trip_ref[...]) + _dot(cp_lo, trip_ref[...])
                   + jnp.sum(cn, axis=-1, keepdims=True))
        wp = jnp.exp2(zp - cp - later_p)
        wn = jnp.where(allowed_n, jnp.exp2(zn - cn - later_n), 0.0)
        outs.append(_dot(wp.astype(BF16), vp) + _dot(wn.astype(BF16), vn_ref[:, sl]))
    o_ref[...] = jnp.concatenate(outs, axis=-1).astype(BF16)


def _cache_head(c_ref, hh):
    past = c_ref.shape[0] // N_HEADS
    return c_ref[pl.ds(hh, past, stride=N_HEADS), :].astype(BF16)


def _sample_specs(t, past, layer):
    new = pl.BlockSpec((None, t, GROUP_W), lambda b: (b, 0, 0))
    cache = pl.BlockSpec((None, None, past * N_HEADS, HEAD_DIM), lambda b: (layer, b, 0, 0))
    return new, cache


def _sb_sample(q, kn, vn, ck, cv, layer):
    nb, t, _ = q.shape
    past = ck.shape[2] // N_HEADS
    new, cache = _sample_specs(t, past, layer)
    trip = jnp.tri(past, k=-1, dtype=BF16)
    trin = jnp.tri(t, k=-1, dtype=BF16)
    return pl.pallas_call(
        functools.partial(_sb_sample_kernel, past=past, t=t),
        grid=(nb,),
        in_specs=[new, new, new, cache, cache,
                  pl.BlockSpec((past, past), lambda b: (0, 0)),
                  pl.BlockSpec((t, t), lambda b: (0, 0))],
        out_specs=new,
        out_shape=jax.ShapeDtypeStruct((nb, t, GROUP_W), BF16),
        compiler_params=_cparams("arbitrary"),
        name="sb_sample",
    )(q, kn, vn, ck, cv, trip, trin)


def _fox_sample_kernel(q_ref, kn_ref, vn_ref, ck_ref, cv_ref, lfp_ref, lfn_ref, trip_ref, trin_ref, o_ref,
                       *, past, t):
    rown = lax.broadcasted_iota(jnp.int32, (t, t), 0)
    coln = lax.broadcasted_iota(jnp.int32, (t, t), 1)
    cum_p = _dot_f32(lfp_ref[...], trip_ref[...])
    total = cum_p[:, past - 1:past]
    bias_p = (total - cum_p) * LOG2E
    bias_n = -_dot_f32(lfn_ref[...], trin_ref[...]) * LOG2E
    outs = []
    for hh in range(N_HEADS):
        sl = slice(hh * HEAD_DIM, (hh + 1) * HEAD_DIM)
        q = q_ref[:, sl]
        sp = _dot_nt(q, _cache_head(ck_ref, hh)) + bias_p[hh:hh + 1, :]
        sn = _dot_nt(q, kn_ref[:, sl]) + bias_n[hh:hh + 1, :]
        sn = jnp.where(coln <= rown, sn, NEG_INF)
        m = jnp.maximum(jnp.max(sp, axis=-1, keepdims=True), jnp.max(sn, axis=-1, keepdims=True))
        pp = jnp.exp2(sp - m)
        pn = jnp.exp2(sn - m)
        inv = 1.0 / (jnp.sum(pp, axis=-1, keepdims=True) + jnp.sum(pn, axis=-1, keepdims=True))
        outs.append(_dot((pp * inv).astype(BF16), _cache_head(cv_ref, hh))
                    + _dot((pn * inv).astype(BF16), vn_ref[:, sl]))
    o_ref[...] = jnp.concatenate(outs, axis=-1).astype(BF16)


def _fox_sample(q, kn, vn, ck, cv, lf_past, lf_new, layer):
    nb, t, _ = q.shape
    past = ck.shape[2] // N_HEADS
    new, cache = _sample_specs(t, past, layer)
    trip = jnp.tri(past, dtype=F32).T
    trin = jnp.tri(t, dtype=F32).T
    return pl.pallas_call(
        functools.partial(_fox_sample_kernel, past=past, t=t),
        grid=(nb,),
        in_specs=[new, new, new, cache, cache,
                  pl.BlockSpec((None, 8, past), lambda b: (b, 0, 0)),
                  pl.BlockSpec((None, 8, t), lambda b: (b, 0, 0)),
                  pl.BlockSpec((past, past), lambda b: (0, 0)),
                  pl.BlockSpec((t, t), lambda b: (0, 0))],
        out_specs=new,
        out_shape=jax.ShapeDtypeStruct((nb, t, GROUP_W), BF16),
        compiler_params=_cparams("arbitrary"),
        name="fox_sample",
    )(q, kn, vn, ck, cv, lf_past, lf_new, trip, trin)


def _diff_sample_kernel(q_ref, kn_ref, vn_ref, ck_ref, cv_ref, slope_ref, lam_ref, sg_ref, o_ref,
                        *, past, t, lam_init):
    lam = _diff_lambda(lam_ref, lam_init)
    q_pos_p = lax.broadcasted_iota(jnp.int32, (t, past), 0) + past
    k_pos_p = lax.broadcasted_iota(jnp.int32, (t, past), 1)
    q_pos_n = lax.broadcasted_iota(jnp.int32, (t, t), 0) + past
    k_pos_n = lax.broadcasted_iota(jnp.int32, (t, t), 1) + past
    vis_p = lax.shift_right_logical(k_pos_p, CHUNK_SHIFT) <= lax.shift_right_logical(q_pos_p, CHUNK_SHIFT)
    vis_n = lax.shift_right_logical(k_pos_n, CHUNK_SHIFT) <= lax.shift_right_logical(q_pos_n, CHUNK_SHIFT)
    dist_p = jnp.abs(q_pos_p - k_pos_p).astype(F32)
    dist_n = jnp.abs(q_pos_n - k_pos_n).astype(F32)
    outs = []
    for hh in range(N_HEADS):
        sl = slice(hh * HEAD_DIM, (hh + 1) * HEAD_DIM)
        slope = slope_ref[hh][:, 0:1] * LOG2E
        bias_p = jnp.where(vis_p, -slope * dist_p, NEG_INF)
        bias_n = jnp.where(vis_n, -slope * dist_n, NEG_INF)
        kp = _cache_head(ck_ref, hh)
        kn = kn_ref[:, sl]
        p_p, p_n = None, None
        for qh, coef in zip(_split_halves(q_ref[:, sl]), (None, lam)):
            sp = _dot_nt(qh, kp) + bias_p
            sn = _dot_nt(qh, kn) + bias_n
            m = jnp.maximum(jnp.max(sp, axis=-1, keepdims=True), jnp.max(sn, axis=-1, keepdims=True))
            ep = jnp.exp2(sp - m)
            en = jnp.exp2(sn - m)
            inv = 1.0 / (jnp.sum(ep, axis=-1, keepdims=True) + jnp.sum(en, axis=-1, keepdims=True))
            if coef is None:
                p_p, p_n = ep * inv, en * inv
            else:
                p_p, p_n = p_p - coef * (ep * inv), p_n - coef * (en * inv)
        o = _dot(p_p.astype(BF16), _cache_head(cv_ref, hh)) + _dot(p_n.astype(BF16), vn_ref[:, sl])
        outs.append(_rms_rows(o, sg_ref[...]) * (1.0 - lam_init))
    o_ref[...] = jnp.concatenate(outs, axis=-1).astype(BF16)


def _diff_sample(q, kn, vn, ck, cv, slopes, lam, subln_g, lam_init, layer):
    nb, t, _ = q.shape
    past = ck.shape[2] // N_HEADS
    new, cache = _sample_specs(t, past, layer)
    return pl.pallas_call(
        functools.partial(_diff_sample_kernel, past=past, t=t, lam_init=lam_init),
        grid=(nb,),
        in_specs=[new, new, new, cache, cache,
                  pl.BlockSpec((N_HEADS, 1, HEAD_DIM), lambda b: (0, 0, 0)),
                  pl.BlockSpec((4, DIFF_QK), lambda b: (0, 0)),
                  pl.BlockSpec((1, HEAD_DIM), lambda b: (0, 0))],
        out_specs=new,
        out_shape=jax.ShapeDtypeStruct((nb, t, GROUP_W), BF16),
        compiler_params=_cparams("arbitrary"),
        name="diff_sample",
    )(q, kn, vn, ck, cv, slopes, lam, subln_g)


def _outproj_kernel(x_ref, osb_ref, ofx_ref, odf_ref, w_ref, o_ref):
    y = _dot(osb_ref[...], w_ref[0:GROUP_W, :])
    y += _dot(ofx_ref[...], w_ref[GROUP_W:2 * GROUP_W, :])
    y += _dot(odf_ref[...], w_ref[2 * GROUP_W:3 * GROUP_W, :])
    o_ref[...] = x_ref[...] + y


def _outproj(x, tile0, o_sb, o_fox, o_diff, w, tm):
    rows = o_sb.shape[0]
    d = x.shape[1]
    row = lambda i: (i, 0)
    xrow = lambda i: (i + tile0, 0)
    slab = pl.BlockSpec((tm, GROUP_W), row)
    return pl.pallas_call(
        _outproj_kernel,
        grid=(rows // tm,),
        in_specs=[pl.BlockSpec((tm, d), xrow), slab, slab, slab, _vmem_whole()],
        out_specs=pl.BlockSpec((tm, d), xrow),
        out_shape=jax.ShapeDtypeStruct(x.shape, F32),
        input_output_aliases={0: 0},
        compiler_params=_cparams("arbitrary"),
        name="outproj",
    )(x, o_sb, o_fox, o_diff, w)


def _swiglu_part(h, wg_ref, wu_ref, wd_ref):
    gate = _dot(h, wg_ref[...])
    up = _dot(h, wu_ref[...])
    a = gate * jax.nn.sigmoid(gate) * up
    return _dot(a.astype(BF16), wd_ref[...])


def _ffn_kernel(x_ref, g_ref, wg_ref, wu_ref, wd_ref, o_ref, h_ref):
    f = pl.program_id(1)

    @pl.when(f == 0)
    def _():
        h_ref[...] = _rms_rows(x_ref[...], g_ref[...]).astype(BF16)

    part = _swiglu_part(h_ref[...], wg_ref, wu_ref, wd_ref)

    @pl.when(f == 0)
    def _():
        o_ref[...] = x_ref[...] + part

    @pl.when(f != 0)
    def _():
        o_ref[...] += part


def _ffn_dense(x, g, wg, wu, wd, tm, tf):
    t, d = x.shape
    nf = wg.shape[1] // tf
    return pl.pallas_call(
        _ffn_kernel,
        grid=(t // tm, nf),
        in_specs=[pl.BlockSpec((tm, d), lambda i, f: (i, 0)),
                  pl.BlockSpec((1, d), lambda i, f: (0, 0)),
                  pl.BlockSpec((d, tf), lambda i, f: (0, f)),
                  pl.BlockSpec((d, tf), lambda i, f: (0, f)),
                  pl.BlockSpec((tf, d), lambda i, f: (f, 0))],
        out_specs=pl.BlockSpec((tm, d), lambda i, f: (i, 0)),
        out_shape=jax.ShapeDtypeStruct((t, d), F32),
        scratch_shapes=[pltpu.VMEM((tm, d), BF16)],
        compiler_params=_cparams("arbitrary", "arbitrary"),
        name="ffn_dense",
    )(x, g, wg, wu, wd)


def _route_kernel(x_ref, g_ref, wr_ref, h_ref, gates_ref, idx_ref):
    hf = _rms_rows(x_ref[...], g_ref[...])
    h_ref[...] = hf.astype(BF16)
    logits = _dot_f32(hf, wr_ref[...])
    lane = lax.broadcasted_iota(jnp.int32, logits.shape, 1)
    m1 = jnp.max(logits, axis=-1, keepdims=True)
    i1 = jnp.min(jnp.where(logits == m1, lane, N_EXPERTS), axis=-1, keepdims=True)
    rest = jnp.where(lane == i1, NEG_INF, logits)
    m2 = jnp.max(rest, axis=-1, keepdims=True)
    i2 = jnp.min(jnp.where(rest == m2, lane, N_EXPERTS), axis=-1, keepdims=True)
    e2 = jnp.exp(m2 - m1)
    inv = 1.0 / (1.0 + e2)
    gates_ref[...] = jnp.where(lane == 0, inv, jnp.where(lane == 1, e2 * inv, 0.0))
    idx_ref[...] = jnp.where(lane == 0, i1, jnp.where(lane == 1, i2, 0))


def _moe_route(x, g, wr, tm):
    t, d = x.shape
    ne = wr.shape[1]
    row = lambda i: (i, 0)
    return pl.pallas_call(
        _route_kernel,
        grid=(t // tm,),
        in_specs=[pl.BlockSpec((tm, d), row), pl.BlockSpec((1, d), lambda i: (0, 0)),
                  pl.BlockSpec((d, ne), lambda i: (0, 0))],
        out_specs=[pl.BlockSpec((tm, d), row), pl.BlockSpec((tm, ne), row), pl.BlockSpec((tm, ne), row)],
        out_shape=[jax.ShapeDtypeStruct((t, d), BF16), jax.ShapeDtypeStruct((t, ne), F32),
                   jax.ShapeDtypeStruct((t, ne), jnp.int32)],
        compiler_params=_cparams("arbitrary"),
        name="moe_route",
    )(x, g, wr)


def _expert_kernel(vt_ref, ve_ref, vf_ref, st_ref, en_ref, xs_ref, wg_ref, wu_ref, wd_ref, o_ref, h_ref, *, tm):
    v = pl.program_id(0)
    f = pl.program_id(1)
    lo = jnp.maximum(st_ref[v], vt_ref[v] * tm)
    hi = jnp.minimum(en_ref[v], (vt_ref[v] + 1) * tm)

    @pl.when(hi > lo)
    def _():
        @pl.when(f == 0)
        def _():
            rows = lax.broadcasted_iota(jnp.int32, (tm, 1), 0) + vt_ref[v] * tm
            mine = (rows >= lo) & (rows < hi)
            h_ref[...] = jnp.where(mine, xs_ref[...], jnp.zeros_like(xs_ref))

        part = _swiglu_part(h_ref[...], wg_ref, wu_ref, wd_ref)
        fresh = (vf_ref[v] == 1) & (f == 0)

        @pl.when(fresh)
        def _():
            o_ref[...] = part

        @pl.when(jnp.logical_not(fresh))
        def _():
            o_ref[...] += part


def _moe_experts(xs, sched, wg, wu, wd, tm, tf):
    r, d = xs.shape
    ne, _, fe = wg.shape
    tpe = fe // tf
    n_visits = sched[0].shape[0]
    grid_spec = pltpu.PrefetchScalarGridSpec(
        num_scalar_prefetch=5,
        grid=(n_visits, tpe),
        in_specs=[pl.BlockSpec((tm, d), lambda v, f, vt, ve, vf, st, en: (vt[v], 0)),
                  pl.BlockSpec((None, d, tf), lambda v, f, vt, ve, vf, st, en: (ve[v], 0, f)),
                  pl.BlockSpec((None, d, tf), lambda v, f, vt, ve, vf, st, en: (ve[v], 0, f)),
                  pl.BlockSpec((None, tf, d), lambda v, f, vt, ve, vf, st, en: (ve[v], f, 0))],
        out_specs=pl.BlockSpec((tm, d), lambda v, f, vt, ve, vf, st, en: (vt[v], 0)),
        scratch_shapes=[pltpu.VMEM((tm, d), BF16)])
    return pl.pallas_call(
        functools.partial(_expert_kernel, tm=tm),
        grid_spec=grid_spec,
        out_shape=jax.ShapeDtypeStruct((r, d), F32),
        compiler_params=_cparams("arbitrary", "arbitrary"),
        name="moe_experts",
    )(*sched, xs, wg, wu, wd)


def _combine_kernel(x_ref, y1_ref, y2_ref, gates_ref, o_ref):
    gates = gates_ref[...]
    o_ref[...] = x_ref[...] + (gates[:, 0:1] * y1_ref[...] + gates[:, 1:2] * y2_ref[...])


def _moe_combine(x, y1, y2, gates, tm):
    t, d = x.shape
    row = lambda i: (i, 0)
    wide = pl.BlockSpec((tm, d), row)
    return pl.pallas_call(
        _combine_kernel,
        grid=(t // tm,),
        in_specs=[wide, wide, wide, pl.BlockSpec((tm, gates.shape[1]), row)],
        out_specs=wide,
        out_shape=jax.ShapeDtypeStruct((t, d), F32),
        compiler_params=_cparams("arbitrary"),
        name="moe_combine",
    )(x, y1, y2, gates)


def _expert_schedule(idx, tm):
    t = idx.shape[0]
    i12 = idx[:, :2]
    order = jnp.argsort(i12.reshape(-1), stable=True).astype(jnp.int32)
    src = order // 2
    onehot = (i12[:, :, None] == jnp.arange(N_EXPERTS, dtype=jnp.int32)).sum(axis=1).astype(jnp.int32)
    csum = jnp.cumsum(onehot, axis=0)
    counts = csum[-1]
    ends = jnp.cumsum(counts)
    starts = ends - counts
    pos = jnp.take_along_axis(starts[None, :] + csum - onehot, i12, axis=1)

    n_tiles = (2 * t) // tm
    n_visits = n_tiles + N_EXPERTS - 1
    first_tile = starts // tm
    nvis = jnp.where(counts > 0, (ends - 1) // tm - first_tile + 1, 0)
    vend = jnp.cumsum(nvis)
    v = jnp.arange(n_visits, dtype=jnp.int32)
    active = v < vend[-1]
    ve = jnp.minimum(jnp.sum(v[:, None] >= vend[None, :], axis=1), N_EXPERTS - 1).astype(jnp.int32)
    vt = jnp.where(active, first_tile[ve] + v - (vend - nvis)[ve], n_tiles - 1).astype(jnp.int32)
    vf = jnp.concatenate([jnp.ones((1,), jnp.int32), (vt[1:] != vt[:-1]).astype(jnp.int32)])
    st = jnp.where(active, starts[ve], 0).astype(jnp.int32)
    en = jnp.where(active, ends[ve], 0).astype(jnp.int32)
    return src, pos, (vt, ve, vf, st, en)


def _ffn_moe(x, g, wg, wu, wd, wr, tm, tf):
    t = x.shape[0]
    tme = _pick_tile(2 * t, (512, 256, 128, 64, 8))
    h, gates, idx = _moe_route(x, g, wr, tm)
    src, pos, sched = _expert_schedule(idx, tme)
    def rows(a, ids):
        return a.at[ids].get(mode="promise_in_bounds")

    ys = _moe_experts(rows(h, src), sched, wg, wu, wd, tme, tf)
    return _moe_combine(x, rows(ys, pos[:, 0]), rows(ys, pos[:, 1]), gates, tm)


def _pick_tile(n, prefs):
    for p in prefs:
        if n % p == 0:
            return p
    return n


def _round_up(n, m):
    return (n + m - 1) // m * m


def kernel(x_prompt, x_sample, cache_sb_k, cache_sb_v, cache_fox_k, cache_fox_v, cache_fox_logf, cache_diff_k, cache_diff_v, norm_mix_g, norm_ffn_g, w_in, b_fox_f, fox_q_g, fox_k_g, diff_q_g, diff_k_g, diff_lam, diff_subln_g, w_out, w_ffn_gate, w_ffn_up, w_ffn_down, w_router, w_moe_gate, w_moe_up, w_moe_down):
    bp, tp, d = x_prompt.shape
    nb, ts, _ = x_sample.shape
    depth = w_in.shape[0]
    past = cache_sb_k.shape[2]
    assert bp == 1
    n_s = nb * ts
    t_all = tp + n_s
    tm_in = math.gcd(256, math.gcd(tp, n_s))
    tm_out = math.gcd(512, math.gcd(tp, n_s))
    tm = _pick_tile(t_all, (512, 256, 128, 64, 8))
    tq_sb, tk_sb = _pick_tile(tp, SB_TILE[:1] + (256, 128)), _pick_tile(tp, SB_TILE[1:] + (128,))
    tq_fx, tk_fx = _pick_tile(tp, FOX_TILE[:1] + (256, 128)), _pick_tile(tp, FOX_TILE[1:] + (256, 128))
    tq_df, tk_df = _pick_tile(tp, DIFF_TILE[:1] + (256, 128)), _pick_tile(tp, DIFF_TILE[1:] + (256, 128))

    w3 = 3 * GROUP_W
    w_in_r = jnp.concatenate(
        [w_in[:, :, :2 * w3], w_in[:, :, 2 * w3 + N_HEADS:], w_in[:, :, 2 * w3:2 * w3 + N_HEADS],
         jnp.zeros((depth, d, HEAD_DIM - N_HEADS), w_in.dtype)], axis=-1).astype(BF16)
    b_f = jnp.pad(b_fox_f, ((0, 0), (0, HEAD_DIM - N_HEADS)))
    w_out_b = w_out.astype(BF16)
    w_fg, w_fu, w_fd = w_ffn_gate.astype(BF16), w_ffn_up.astype(BF16), w_ffn_down.astype(BF16)
    ff = w_ffn_gate.shape[-1]
    tf = _pick_tile(ff, (512, 256, 128))
    ffe = w_moe_gate.shape[-1]
    tfe = 512 if ffe >= 512 else 128
    pad_e = _round_up(ffe, tfe) - ffe
    w_mg = jnp.pad(w_moe_gate, ((0, 0), (0, 0), (0, 0), (0, pad_e))).astype(BF16)
    w_mu = jnp.pad(w_moe_up, ((0, 0), (0, 0), (0, 0), (0, pad_e))).astype(BF16)
    w_md = jnp.pad(w_moe_down, ((0, 0), (0, 0), (0, pad_e), (0, 0))).astype(BF16)

    slopes = jnp.exp2(-8.0 * jnp.arange(1, N_HEADS + 1, dtype=F32) / N_HEADS)
    slopes = jnp.broadcast_to(slopes[:, None, None], (N_HEADS, 1, HEAD_DIM))

    c_sbk, c_sbv, c_fxk, c_fxv, c_dfk, c_dfv = [
        c.reshape(depth, nb, past * N_HEADS, HEAD_DIM)
        for c in (cache_sb_k, cache_sb_v, cache_fox_k, cache_fox_v, cache_diff_k, cache_diff_v)]

    x = jnp.concatenate([x_prompt.reshape(tp, d), x_sample.reshape(n_s, d)], axis=0)
    stacked_p, stacked_s = (), ()
    logf_p, logf_s = [], []
    for l in range(depth):
        lam_init = 0.8 - 0.6 * math.exp(-0.3 * l)
        dqg = jnp.tile(diff_q_g[l], 2)[None, :]
        dkg = jnp.tile(diff_k_g[l], 2)[None, :]
        proj_args = (norm_mix_g[l][None, :], w_in_r[l], b_f[l][None, :],
                     fox_q_g[l][None, :], fox_k_g[l][None, :], dqg, dkg, tm_in)
        sg = diff_subln_g[l][None, :]
        lam = diff_lam[l]

        res = _inproj(x, 0, tp, l, depth, stacked_p, *proj_args, True)
        sbq, sbk, sbv, fxq, fxk, fxv, dfq, dfk, dfv = res[:9]
        stacked_p = tuple(res[9:9 + N_STACKED])
        logf_p.append(res[15][:, :N_HEADS])
        cum_k = res[16][:, :N_HEADS].T.reshape(N_HEADS, tp // tk_fx, 1, tk_fx)
        o_sb = _sb_prompt(sbq, sbk, sbv, tp, tq_sb, tk_sb)
        o_fx = _fox_prompt(fxq, fxk, fxv, cum_k, tp, tq_fx, tk_fx)
        o_df = _diff_prompt(dfq, dfk, dfv, slopes, lam, sg, tp, tq_df, tk_df, lam_init)
        x = _outproj(x, 0, o_sb, o_fx, o_df, w_out_b[l], tm_out)

        res = _inproj(x, tp // tm_in, n_s, l, depth, stacked_s, *proj_args, False)
        sbq, sbk, sbv, fxq, fxk, fxv, dfq, dfk, dfv = [a.reshape(nb, ts, GROUP_W) for a in res[:9]]
        stacked_s = tuple(res[9:9 + N_STACKED])
        lf_s = res[15][:, :N_HEADS]
        logf_s.append(lf_s)
        lf_new = jnp.pad(jnp.swapaxes(lf_s.reshape(nb, ts, N_HEADS), 1, 2), ((0, 0), (0, 8 - N_HEADS), (0, 0)))
        lf_past = jnp.pad(jnp.swapaxes(cache_fox_logf[l], 1, 2), ((0, 0), (0, 8 - N_HEADS), (0, 0)))
        o_sb = _sb_sample(sbq, sbk, sbv, c_sbk, c_sbv, l)
        o_fx = _fox_sample(fxq, fxk, fxv, c_fxk, c_fxv, lf_past, lf_new, l)
        o_df = _diff_sample(dfq, dfk, dfv, c_dfk, c_dfv, slopes, lam, sg, lam_init, l)
        x = _outproj(x, tp // tm_out, o_sb.reshape(n_s, GROUP_W), o_fx.reshape(n_s, GROUP_W),
                     o_df.reshape(n_s, GROUP_W), w_out_b[l], tm_out)

        if l % 2 == 0:
            x = _ffn_dense(x, norm_ffn_g[l][None, :], w_fg[l // 2], w_fu[l // 2], w_fd[l // 2], tm, tf)
        else:
            x = _ffn_moe(x, norm_ffn_g[l][None, :], w_mg[l // 2], w_mu[l // 2], w_md[l // 2],
                         w_router[l // 2], tm, tfe)

    kv_tail = (N_HEADS, HEAD_DIM)
    sbk_p, sbv_p, fxk_p, fxv_p, dfk_p, dfv_p = [a.reshape((depth, 1, tp) + kv_tail) for a in stacked_p]
    sbk_s, sbv_s, fxk_s, fxv_s, dfk_s, dfv_s = [a.reshape((depth, nb, ts) + kv_tail) for a in stacked_s]
    lf_p = jnp.stack(logf_p).reshape(depth, 1, tp, N_HEADS)
    lf_s = jnp.stack(logf_s).reshape(depth, nb, ts, N_HEADS)
    return (x[:tp].reshape(1, tp, d), x[tp:].reshape(nb, ts, d),
            sbk_p, sbv_p, fxk_p, fxv_p, lf_p, dfk_p, dfv_p,
            sbk_s, sbv_s, fxk_s, fxv_s, lf_s, dfk_s, dfv_s)
```

```python
import functools
import math

import jax
import jax.numpy as jnp
from jax import lax
from jax.experimental import pallas as pl
from jax.experimental.pallas import tpu as pltpu

F32 = jnp.float32
BF16 = jnp.bfloat16

LANES = 128
HEAD_DIM = 128
N_HEADS = 4
GROUP_W = N_HEADS * HEAD_DIM
DIFF_QK = HEAD_DIM // 2
CHUNK = 64
CHUNK_SHIFT = 6
N_EXPERTS = 8
RMS_EPS = 1e-6
LOG2E = math.log2(math.e)
VMEM_LIMIT = 56 * 1024 * 1024
SB_TILE = (2048, 256)
FOX_TILE = (2048, 1024)
DIFF_TILE = (2048, 512)
NEG_INF = float("-inf")
SIGN_BIT = -2 ** 31
HIGHEST = lax.Precision.HIGHEST


def _cparams(*sem):
    return pltpu.CompilerParams(dimension_semantics=sem, vmem_limit_bytes=VMEM_LIMIT)


def _vmem_whole():
    return pl.BlockSpec(memory_space=pltpu.VMEM)


def _any_space():
    return pl.BlockSpec(memory_space=pl.ANY)


def _rms_rows(x, g):
    return x * lax.rsqrt(jnp.mean(x * x, axis=-1, keepdims=True) + RMS_EPS) * g


def _log_sigmoid(x):
    return -(jnp.maximum(-x, 0.0) + jnp.log1p(jnp.exp(-jnp.abs(x))))


def _softplus2(z):
    neg_abs = lax.bitcast_convert_type(lax.bitcast_convert_type(z, jnp.int32) | SIGN_BIT, F32)
    return jnp.maximum(z, 0.0) + jnp.log2(1.0 + jnp.exp2(neg_abs))


def _dot(a, b):
    return jnp.dot(a, b, preferred_element_type=F32)


def _dot_f32(a, b):
    return jnp.dot(a, b, preferred_element_type=F32, precision=HIGHEST)


def _dot_nt(a, b):
    return lax.dot_general(a, b, (((1,), (1,)), ((), ())), preferred_element_type=F32)


def _lane_blocks(x):
    return [x[:, c * LANES:(c + 1) * LANES] for c in range(x.shape[1] // LANES)]


def _sum_list(xs):
    acc = xs[0]
    for x in xs[1:]:
        acc = acc + x
    return acc


def _max_list(xs):
    acc = xs[0]
    for x in xs[1:]:
        acc = jnp.maximum(acc, x)
    return acc


def _head_rms(z, g, group):
    outs = []
    for hh in range(N_HEADS):
        zz = z[:, hh * HEAD_DIM:(hh + 1) * HEAD_DIM]
        sq = zz * zz
        if group == HEAD_DIM:
            r = lax.rsqrt(jnp.mean(sq, axis=-1, keepdims=True) + RMS_EPS)
        else:
            lo = lax.broadcasted_iota(jnp.int32, zz.shape, 1) < DIFF_QK
            s_lo = jnp.sum(jnp.where(lo, sq, 0.0), axis=-1, keepdims=True)
            s_hi = jnp.sum(jnp.where(lo, 0.0, sq), axis=-1, keepdims=True)
            r = jnp.where(lo, lax.rsqrt(s_lo / DIFF_QK + RMS_EPS), lax.rsqrt(s_hi / DIFF_QK + RMS_EPS))
        outs.append(zz * r * g)
    return jnp.concatenate(outs, axis=-1)


N_STACKED = 6


def _store_heads(o_ref, z):
    rows = z.shape[0]
    for hh in range(N_HEADS):
        o_ref[pl.ds(hh, rows, stride=N_HEADS), :] = z[:, hh * HEAD_DIM:(hh + 1) * HEAD_DIM]


def _inproj_kernel(*refs, with_cumsum, aliased):
    x_ref, g_ref, w_ref, bf_ref, fqg_ref, fkg_ref, dqg_ref, dkg_ref, tri_ref = refs[:9]
    outs = refs[9 + (N_STACKED if aliased else 0):]
    (sbq, sbk, sbv, fxq, fxk, fxv, dfq, dfk, dfv,
     sbk32, sbv32, fxk32, fxv32, dfk32, dfv32, logf, cumf) = outs[:17]
    h = _rms_rows(x_ref[...], g_ref[...]).astype(BF16)

    def grp(c):
        return _dot(h, w_ref[:, c * GROUP_W:(c + 1) * GROUP_W])

    z = grp(0)
    sbq[...] = (z * (HEAD_DIM ** -0.5 * LOG2E)).astype(BF16)
    z = grp(1)
    _store_heads(sbk32, z)
    sbk[...] = z.astype(BF16)
    z = grp(2)
    _store_heads(sbv32, z)
    sbv[...] = z.astype(BF16)

    z = _head_rms(grp(3), fqg_ref[...], HEAD_DIM)
    fxq[...] = (z * (HEAD_DIM ** -0.5 * LOG2E)).astype(BF16)
    z = _head_rms(grp(4), fkg_ref[...], HEAD_DIM)
    _store_heads(fxk32, z)
    fxk[...] = z.astype(BF16)
    z = grp(5)
    _store_heads(fxv32, z)
    fxv[...] = z.astype(BF16)

    z = _head_rms(grp(6), dqg_ref[...], DIFF_QK)
    dfq[...] = (z * (DIFF_QK ** -0.5 * LOG2E)).astype(BF16)
    z = _head_rms(grp(7), dkg_ref[...], DIFF_QK)
    _store_heads(dfk32, z)
    dfk[...] = z.astype(BF16)
    z = grp(8)
    _store_heads(dfv32, z)
    dfv[...] = z.astype(BF16)

    zf = _dot(h, w_ref[:, 9 * GROUP_W:9 * GROUP_W + HEAD_DIM])
    lf = _log_sigmoid(zf + bf_ref[...])
    logf[...] = lf
    if with_cumsum:
        carry_ref = outs[17]

        @pl.when(pl.program_id(0) == 0)
        def _():
            carry_ref[...] = jnp.zeros_like(carry_ref)

        cum = _dot_f32(tri_ref[...], lf) + carry_ref[0:1, :]
        cumf[...] = cum
        tm = cum.shape[0]
        carry_ref[...] = jnp.broadcast_to(cum[tm - 1:tm, :], carry_ref.shape)
    else:
        cumf[...] = lf


def _inproj(x, tile0, rows, layer, depth, stacked, g, w, bf, fqg, fkg, dqg, dkg, tm, with_cumsum):
    d = x.shape[1]
    aliased = layer > 0
    row = lambda i: (i, 0)
    full = lambda i: (0, 0)
    small = pl.BlockSpec((1, HEAD_DIM), full)
    slab = pl.BlockSpec((tm, GROUP_W), row)
    narrow = pl.BlockSpec((tm, HEAD_DIM), row)
    stacked_spec = pl.BlockSpec((None, tm * N_HEADS, HEAD_DIM), lambda i: (layer, i, 0))
    tri = jnp.tri(tm, dtype=F32)
    in_specs = [pl.BlockSpec((tm, d), lambda i: (i + tile0, 0)), pl.BlockSpec((1, d), full), _vmem_whole(),
                small, small, small, small, small, pl.BlockSpec((tm, tm), full)]
    args = [x, g, w, bf, fqg, fkg, dqg, dkg, tri]
    aliases = {}
    if aliased:
        in_specs += [_any_space()] * N_STACKED
        aliases = {len(args) + k: 9 + k for k in range(N_STACKED)}
        args += list(stacked)
    out_bf = jax.ShapeDtypeStruct((rows, GROUP_W), BF16)
    out_n = jax.ShapeDtypeStruct((rows, HEAD_DIM), F32)
    out_st = jax.ShapeDtypeStruct((depth, rows * N_HEADS, HEAD_DIM), F32)
    return pl.pallas_call(
        functools.partial(_inproj_kernel, with_cumsum=with_cumsum, aliased=aliased),
        grid=(rows // tm,),
        in_specs=in_specs,
        out_specs=[slab] * 9 + [stacked_spec] * N_STACKED + [narrow, narrow],
        out_shape=[out_bf] * 9 + [out_st] * N_STACKED + [out_n, out_n],
        scratch_shapes=[pltpu.VMEM((8, HEAD_DIM), F32)] if with_cumsum else [],
        input_output_aliases=aliases,
        compiler_params=_cparams("arbitrary"),
        name="inproj",
    )(*args)


def _sb_prompt_kernel(q_ref, k_ref, v_ref, tri_ref, o_ref, c_ref, acc_ref, *, tq, tk):
    i = pl.program_id(1)
    ratio = tq // tk
    row = lax.broadcasted_iota(jnp.int32, (tk, tk), 0)
    col = lax.broadcasted_iota(jnp.int32, (tk, tk), 1)
    strictly_before = col < row
    c_ref[...] = jnp.zeros_like(c_ref)
    acc_ref[...] = jnp.zeros_like(acc_ref)

    def step(j, r0, r1, masked):
        kc, vc = _kv_chunk(k_ref, v_ref, j, tk)
        z = _dot_nt(q_ref[r0:r1, :], kc)
        cost = _softplus2(z)
        if masked:
            cost = jnp.where(strictly_before, cost, 0.0)
        incl = _dot(cost.astype(BF16), tri_ref[...])
        w = jnp.exp2(z - incl)
        if masked:
            w = jnp.where(strictly_before, w, 0.0)
        carried = c_ref[r0:r1, :]
        acc_ref[r0:r1, :] += jnp.exp2(-carried) * _dot(w.astype(BF16), vc)
        c_ref[r0:r1, :] = carried + incl[:, 0:1]

    for d in reversed(range(ratio)):
        step(i * ratio + d, d * tk, (d + 1) * tk, True)
        if d + 1 < ratio:
            step(i * ratio + d, (d + 1) * tk, tq, False)

    def body(jj, carry):
        step(i * ratio - 1 - jj, 0, tq, False)
        return carry

    lax.fori_loop(0, i * ratio, body, 0)
    o_ref[...] = acc_ref[...].astype(BF16)


def _sb_prompt(q, k, v, tp, tq, tk):
    tri = jnp.tri(tk, dtype=BF16)
    return pl.pallas_call(
        functools.partial(_sb_prompt_kernel, tq=tq, tk=tk),
        grid=(N_HEADS, tp // tq),
        in_specs=[pl.BlockSpec((tq, HEAD_DIM), lambda h, i: (i, h)),
                  _kv_spec(tp), _kv_spec(tp),
                  pl.BlockSpec((tk, tk), lambda h, i: (0, 0))],
        out_specs=pl.BlockSpec((tq, HEAD_DIM), lambda h, i: (i, h)),
        out_shape=jax.ShapeDtypeStruct((tp, GROUP_W), BF16),
        scratch_shapes=[pltpu.VMEM((tq, LANES), F32), pltpu.VMEM((tq, HEAD_DIM), F32)],
        compiler_params=_cparams("arbitrary", "arbitrary"),
        name="sb_prompt",
    )(q, k, v, tri)


def _softmax_step(s, vc, m_ref, l_ref, acc_ref, r0, r1):
    blocks = _lane_blocks(s)
    m_prev = m_ref[r0:r1, :]
    m_new = jnp.maximum(m_prev, jnp.max(_max_list(blocks), axis=-1, keepdims=True))
    alpha = jnp.exp2(m_prev - m_new)
    ps = [jnp.exp2(b - m_new) for b in blocks]
    l_ref[r0:r1, :] = alpha * l_ref[r0:r1, :] + _sum_list(ps)
    acc_ref[r0:r1, :] = alpha * acc_ref[r0:r1, :] + _dot(jnp.concatenate(ps, axis=1).astype(BF16), vc)
    m_ref[r0:r1, :] = m_new


def _softmax_init(m_ref, l_ref, acc_ref):
    m_ref[...] = jnp.full_like(m_ref, NEG_INF)
    l_ref[...] = jnp.zeros_like(l_ref)
    acc_ref[...] = jnp.zeros_like(acc_ref)


def _softmax_result(l_ref, acc_ref):
    return acc_ref[...] * (1.0 / jnp.sum(l_ref[...], axis=-1, keepdims=True))


def _diagonal_steps(step, n_full, ratio, tq, tk):
    for d in range(ratio):
        step(n_full + d, d * tk, (d + 1) * tk, True)
        if d + 1 < ratio:
            step(n_full + d, (d + 1) * tk, tq, False)


def _kv_chunk(k_ref, v_ref, j, tk):
    start = pl.multiple_of(j * tk, tk)
    return k_ref[pl.ds(start, tk), :], v_ref[pl.ds(start, tk), :]


def _fox_prompt_kernel(q_ref, k_ref, v_ref, fk_ref, o_ref, m_ref, l_ref, acc_ref, *, tq, tk):
    i = pl.program_id(1)
    ratio = tq // tk
    n_full = i * ratio
    f0 = fk_ref[n_full][:, 0:1]
    _softmax_init(m_ref, l_ref, acc_ref)
    row = lax.broadcasted_iota(jnp.int32, (tk, tk), 0)
    col = lax.broadcasted_iota(jnp.int32, (tk, tk), 1)

    def step(j, r0, r1, masked):
        kc, vc = _kv_chunk(k_ref, v_ref, j, tk)
        s = _dot_nt(q_ref[r0:r1, :], kc) + (f0 - fk_ref[j]) * LOG2E
        if masked:
            s = jnp.where(col <= row, s, NEG_INF)
        _softmax_step(s, vc, m_ref, l_ref, acc_ref, r0, r1)

    _diagonal_steps(step, n_full, ratio, tq, tk)

    def body(j, carry):
        step(j, 0, tq, False)
        return carry

    lax.fori_loop(0, n_full, body, 0)
    o_ref[...] = _softmax_result(l_ref, acc_ref).astype(BF16)


def _kv_spec(tp):
    return pl.BlockSpec((tp, HEAD_DIM), lambda h, i: (0, h), pipeline_mode=pl.Buffered(1))


def _fox_prompt(q, k, v, fk, tp, tq, tk):
    stat = lambda: pltpu.VMEM((tq, LANES), F32)
    return pl.pallas_call(
        functools.partial(_fox_prompt_kernel, tq=tq, tk=tk),
        grid=(N_HEADS, tp // tq),
        in_specs=[pl.BlockSpec((tq, HEAD_DIM), lambda h, i: (i, h)),
                  _kv_spec(tp), _kv_spec(tp),
                  pl.BlockSpec((None, tp // tk, 1, tk), lambda h, i: (h, 0, 0, 0))],
        out_specs=pl.BlockSpec((tq, HEAD_DIM), lambda h, i: (i, h)),
        out_shape=jax.ShapeDtypeStruct((tp, GROUP_W), BF16),
        scratch_shapes=[stat(), stat(), pltpu.VMEM((tq, HEAD_DIM), F32)],
        compiler_params=_cparams("arbitrary", "arbitrary"),
        name="fox_prompt",
    )(q, k, v, fk)


def _diff_lambda(lam_ref, lam_init):
    lv = lam_ref[...]
    a = jnp.sum(lv[0:1, :] * lv[1:2, :], axis=-1, keepdims=True)
    b = jnp.sum(lv[2:3, :] * lv[3:4, :], axis=-1, keepdims=True)
    return jnp.exp(a) - jnp.exp(b) + lam_init


def _split_halves(q):
    first = lax.broadcasted_iota(jnp.int32, q.shape, 1) < DIFF_QK
    zero = jnp.zeros_like(q)
    return jnp.where(first, q, zero), jnp.where(first, zero, q)


def _diff_prompt_kernel(q_ref, k_ref, v_ref, slope_ref, lam_ref, sg_ref, o_ref,
                        m1_ref, l1_ref, a1_ref, m2_ref, l2_ref, a2_ref, *, tq, tk, lam_init):
    i = pl.program_id(1)
    ratio = tq // tk
    n_full = i * ratio
    slope = slope_ref[:, 0:1] * LOG2E
    _softmax_init(m1_ref, l1_ref, a1_ref)
    _softmax_init(m2_ref, l2_ref, a2_ref)
    row = lax.broadcasted_iota(jnp.int32, (tk, tk), 0)
    col = lax.broadcasted_iota(jnp.int32, (tk, tk), 1)
    visible = lax.shift_right_logical(col, CHUNK_SHIFT) <= lax.shift_right_logical(row, CHUNK_SHIFT)
    own_rel = row - jnp.abs(row - col)
    col_row = lax.broadcasted_iota(jnp.int32, (1, tk), 1)
    q1_all, q2_all = _split_halves(q_ref[...])

    def step(j, r0, r1, masked):
        kc, vc = _kv_chunk(k_ref, v_ref, j, tk)
        q1, q2 = q1_all[r0:r1, :], q2_all[r0:r1, :]
        first_key = j * tk - i * tq
        if masked:
            bias = jnp.where(visible, slope * (own_rel + first_key).astype(F32), NEG_INF)
        else:
            bias = slope * (col_row + first_key).astype(F32)
        _softmax_step(_dot_nt(q1, kc) + bias, vc, m1_ref, l1_ref, a1_ref, r0, r1)
        _softmax_step(_dot_nt(q2, kc) + bias, vc, m2_ref, l2_ref, a2_ref, r0, r1)

    _diagonal_steps(step, n_full, ratio, tq, tk)

    def body(j, carry):
        step(j, 0, tq, False)
        return carry

    lax.fori_loop(0, n_full, body, 0)
    lam = _diff_lambda(lam_ref, lam_init)
    o = _softmax_result(l1_ref, a1_ref) - lam * _softmax_result(l2_ref, a2_ref)
    o_ref[...] = (_rms_rows(o, sg_ref[...]) * (1.0 - lam_init)).astype(BF16)


def _diff_prompt(q, k, v, slopes, lam, subln_g, tp, tq, tk, lam_init):
    stat = lambda: pltpu.VMEM((tq, LANES), F32)
    acc = lambda: pltpu.VMEM((tq, HEAD_DIM), F32)
    return pl.pallas_call(
        functools.partial(_diff_prompt_kernel, tq=tq, tk=tk, lam_init=lam_init),
        grid=(N_HEADS, tp // tq),
        in_specs=[pl.BlockSpec((tq, HEAD_DIM), lambda h, i: (i, h)),
                  _kv_spec(tp), _kv_spec(tp),
                  pl.BlockSpec((None, 1, HEAD_DIM), lambda h, i: (h, 0, 0)),
                  pl.BlockSpec((4, DIFF_QK), lambda h, i: (0, 0)),
                  pl.BlockSpec((1, HEAD_DIM), lambda h, i: (0, 0))],
        out_specs=pl.BlockSpec((tq, HEAD_DIM), lambda h, i: (i, h)),
        out_shape=jax.ShapeDtypeStruct((tp, GROUP_W), BF16),
        scratch_shapes=[stat(), stat(), acc(), stat(), stat(), acc()],
        compiler_params=_cparams("arbitrary", "arbitrary"),
        name="diff_prompt",
    )(q, k, v, slopes, lam, subln_g)


def _split_bf16(x):
    hi = x.astype(BF16)
    return hi, (x - hi.astype(F32)).astype(BF16)


def _sb_sample_kernel(q_ref, kn_ref, vn_ref, ck_ref, cv_ref, trip_ref, trin_ref, o_ref, *, past, t):
    rown = lax.broadcasted_iota(jnp.int32, (t, t), 0)
    coln = lax.broadcasted_iota(jnp.int32, (t, t), 1)
    allowed_n = coln < rown
    outs = []
    for hh in range(N_HEADS):
        sl = slice(hh * HEAD_DIM, (hh + 1) * HEAD_DIM)
        q = q_ref[:, sl]
        kp = _cache_head(ck_ref, hh)
        vp = _cache_head(cv_ref, hh)
        zp = _dot_nt(q, kp)
        zn = _dot_nt(q, kn_ref[:, sl])
        cp = _softplus2(zp)
        cn = jnp.where(allowed_n, _softplus2(zn), 0.0)
        cn_hi, cn_lo = _split_bf16(cn)
        cp_hi, cp_lo = _split_bf16(cp)
        later_n = _dot(cn_hi, trin_ref[...]) + _dot(cn_lo, trin_ref[...])
        later_p = (_dot(cp_hi, trip_ref[...]) + _dot(cp_lo, trip_ref[...])
                   + jnp.sum(cn, axis=-1, keepdims=True))
        wp = jnp.exp2(zp - cp - later_p)
        wn = jnp.where(allowed_n, jnp.exp2(zn - cn - later_n), 0.0)
        outs.append(_dot(wp.astype(BF16), vp) + _dot(wn.astype(BF16), vn_ref[:, sl]))
    o_ref[...] = jnp.concatenate(outs, axis=-1).astype(BF16)


def _cache_head(c_ref, hh):
    past = c_ref.shape[0] // N_HEADS
    return c_ref[pl.ds(hh, past, stride=N_HEADS), :].astype(BF16)


def _sample_specs(t, past, layer):
    new = pl.BlockSpec((None, t, GROUP_W), lambda b: (b, 0, 0))
    cache = pl.BlockSpec((None, None, past * N_HEADS, HEAD_DIM), lambda b: (layer, b, 0, 0))
    return new, cache


def _sb_sample(q, kn, vn, ck, cv, layer):
    nb, t, _ = q.shape
    past = ck.shape[2] // N_HEADS
    new, cache = _sample_specs(t, past, layer)
    trip = jnp.tri(past, k=-1, dtype=BF16)
    trin = jnp.tri(t, k=-1, dtype=BF16)
    return pl.pallas_call(
        functools.partial(_sb_sample_kernel, past=past, t=t),
        grid=(nb,),
        in_specs=[new, new, new, cache, cache,
                  pl.BlockSpec((past, past), lambda b: (0, 0)),
                  pl.BlockSpec((t, t), lambda b: (0, 0))],
        out_specs=new,
        out_shape=jax.ShapeDtypeStruct((nb, t, GROUP_W), BF16),
        compiler_params=_cparams("arbitrary"),
        name="sb_sample",
    )(q, kn, vn, ck, cv, trip, trin)


def _fox_sample_kernel(q_ref, kn_ref, vn_ref, ck_ref, cv_ref, lfp_ref, lfn_ref, trip_ref, trin_ref, o_ref,
                       *, past, t):
    rown = lax.broadcasted_iota(jnp.int32, (t, t), 0)
    coln = lax.broadcasted_iota(jnp.int32, (t, t), 1)
    cum_p = _dot_f32(lfp_ref[...], trip_ref[...])
    total = cum_p[:, past - 1:past]
    bias_p = (total - cum_p) * LOG2E
    bias_n = -_dot_f32(lfn_ref[...], trin_ref[...]) * LOG2E
    outs = []
    for hh in range(N_HEADS):
        sl = slice(hh * HEAD_DIM, (hh + 1) * HEAD_DIM)
        q = q_ref[:, sl]
        sp = _dot_nt(q, _cache_head(ck_ref, hh)) + bias_p[hh:hh + 1, :]
        sn = _dot_nt(q, kn_ref[:, sl]) + bias_n[hh:hh + 1, :]
        sn = jnp.where(coln <= rown, sn, NEG_INF)
        m = jnp.maximum(jnp.max(sp, axis=-1, keepdims=True), jnp.max(sn, axis=-1, keepdims=True))
        pp = jnp.exp2(sp - m)
        pn = jnp.exp2(sn - m)
        inv = 1.0 / (jnp.sum(pp, axis=-1, keepdims=True) + jnp.sum(pn, axis=-1, keepdims=True))
        outs.append(_dot((pp * inv).astype(BF16), _cache_head(cv_ref, hh))
                    + _dot((pn * inv).astype(BF16), vn_ref[:, sl]))
    o_ref[...] = jnp.concatenate(outs, axis=-1).astype(BF16)


def _fox_sample(q, kn, vn, ck, cv, lf_past, lf_new, layer):
    nb, t, _ = q.shape
    past = ck.shape[2] // N_HEADS
    new, cache = _sample_specs(t, past, layer)
    trip = jnp.tri(past, dtype=F32).T
    trin = jnp.tri(t, dtype=F32).T
    return pl.pallas_call(
        functools.partial(_fox_sample_kernel, past=past, t=t),
        grid=(nb,),
        in_specs=[new, new, new, cache, cache,
                  pl.BlockSpec((None, 8, past), lambda b: (b, 0, 0)),
                  pl.BlockSpec((None, 8, t), lambda b: (b, 0, 0)),
                  pl.BlockSpec((past, past), lambda b: (0, 0)),
                  pl.BlockSpec((t, t), lambda b: (0, 0))],
        out_specs=new,
        out_shape=jax.ShapeDtypeStruct((nb, t, GROUP_W), BF16),
        compiler_params=_cparams("arbitrary"),
        name="fox_sample",
    )(q, kn, vn, ck, cv, lf_past, lf_new, trip, trin)


def _diff_sample_kernel(q_ref, kn_ref, vn_ref, ck_ref, cv_ref, slope_ref, lam_ref, sg_ref, o_ref,
                        *, past, t, lam_init):
    lam = _diff_lambda(lam_ref, lam_init)
    q_pos_p = lax.broadcasted_iota(jnp.int32, (t, past), 0) + past
    k_pos_p = lax.broadcasted_iota(jnp.int32, (t, past), 1)
    q_pos_n = lax.broadcasted_iota(jnp.int32, (t, t), 0) + past
    k_pos_n = lax.broadcasted_iota(jnp.int32, (t, t), 1) + past
    vis_p = lax.shift_right_logical(k_pos_p, CHUNK_SHIFT) <= lax.shift_right_logical(q_pos_p, CHUNK_SHIFT)
    vis_n = lax.shift_right_logical(k_pos_n, CHUNK_SHIFT) <= lax.shift_right_logical(q_pos_n, CHUNK_SHIFT)
    dist_p = jnp.abs(q_pos_p - k_pos_p).astype(F32)
    dist_n = jnp.abs(q_pos_n - k_pos_n).astype(F32)
    outs = []
    for hh in range(N_HEADS):
        sl = slice(hh * HEAD_DIM, (hh + 1) * HEAD_DIM)
        slope = slope_ref[hh][:, 0:1] * LOG2E
        bias_p = jnp.where(vis_p, -slope * dist_p, NEG_INF)
        bias_n = jnp.where(vis_n, -slope * dist_n, NEG_INF)
        kp = _cache_head(ck_ref, hh)
        kn = kn_ref[:, sl]
        p_p, p_n = None, None
        for qh, coef in zip(_split_halves(q_ref[:, sl]), (None, lam)):
            sp = _dot_nt(qh, kp) + bias_p
            sn = _dot_nt(qh, kn) + bias_n
            m = jnp.maximum(jnp.max(sp, axis=-1, keepdims=True), jnp.max(sn, axis=-1, keepdims=True))
            ep = jnp.exp2(sp - m)
            en = jnp.exp2(sn - m)
            inv = 1.0 / (jnp.sum(ep, axis=-1, keepdims=True) + jnp.sum(en, axis=-1, keepdims=True))
            if coef is None:
                p_p, p_n = ep * inv, en * inv
            else:
                p_p, p_n = p_p - coef * (ep * inv), p_n - coef * (en * inv)
        o = _dot(p_p.astype(BF16), _cache_head(cv_ref, hh)) + _dot(p_n.astype(BF16), vn_ref[:, sl])
        outs.append(_rms_rows(o, sg_ref[...]) * (1.0 - lam_init))
    o_ref[...] = jnp.concatenate(outs, axis=-1).astype(BF16)


def _diff_sample(q, kn, vn, ck, cv, slopes, lam, subln_g, lam_init, layer):
    nb, t, _ = q.shape
    past = ck.shape[2] // N_HEADS
    new, cache = _sample_specs(t, past, layer)
    return pl.pallas_call(
        functools.partial(_diff_sample_kernel, past=past, t=t, lam_init=lam_init),
        grid=(nb,),
        in_specs=[new, new, new, cache, cache,
                  pl.BlockSpec((N_HEADS, 1, HEAD_DIM), lambda b: (0, 0, 0)),
                  pl.BlockSpec((4, DIFF_QK), lambda b: (0, 0)),
                  pl.BlockSpec((1, HEAD_DIM), lambda b: (0, 0))],
        out_specs=new,
        out_shape=jax.ShapeDtypeStruct((nb, t, GROUP_W), BF16),
        compiler_params=_cparams("arbitrary"),
        name="diff_sample",
    )(q, kn, vn, ck, cv, slopes, lam, subln_g)


def _outproj_kernel(x_ref, osb_ref, ofx_ref, odf_ref, w_ref, o_ref):
    y = _dot(osb_ref[...], w_ref[0:GROUP_W, :])
    y += _dot(ofx_ref[...], w_ref[GROUP_W:2 * GROUP_W, :])
    y += _dot(odf_ref[...], w_ref[2 * GROUP_W:3 * GROUP_W, :])
    o_ref[...] = x_ref[...] + y


def _outproj(x, tile0, o_sb, o_fox, o_diff, w, tm):
    rows = o_sb.shape[0]
    d = x.shape[1]
    row = lambda i: (i, 0)
    xrow = lambda i: (i + tile0, 0)
    slab = pl.BlockSpec((tm, GROUP_W), row)
    return pl.pallas_call(
        _outproj_kernel,
        grid=(rows // tm,),
        in_specs=[pl.BlockSpec((tm, d), xrow), slab, slab, slab, _vmem_whole()],
        out_specs=pl.BlockSpec((tm, d), xrow),
        out_shape=jax.ShapeDtypeStruct(x.shape, F32),
        input_output_aliases={0: 0},
        compiler_params=_cparams("arbitrary"),
        name="outproj",
    )(x, o_sb, o_fox, o_diff, w)


def _swiglu_part(h, wg_ref, wu_ref, wd_ref):
    gate = _dot(h, wg_ref[...])
    up = _dot(h, wu_ref[...])
    a = gate * jax.nn.sigmoid(gate) * up
    return _dot(a.astype(BF16), wd_ref[...])


def _ffn_kernel(x_ref, g_ref, wg_ref, wu_ref, wd_ref, o_ref, h_ref):
    f = pl.program_id(1)

    @pl.when(f == 0)
    def _():
        h_ref[...] = _rms_rows(x_ref[...], g_ref[...]).astype(BF16)

    part = _swiglu_part(h_ref[...], wg_ref, wu_ref, wd_ref)

    @pl.when(f == 0)
    def _():
        o_ref[...] = x_ref[...] + part

    @pl.when(f != 0)
    def _():
        o_ref[...] += part


def _ffn_dense(x, g, wg, wu, wd, tm, tf):
    t, d = x.shape
    nf = wg.shape[1] // tf
    return pl.pallas_call(
        _ffn_kernel,
        grid=(t // tm, nf),
        in_specs=[pl.BlockSpec((tm, d), lambda i, f: (i, 0)),
                  pl.BlockSpec((1, d), lambda i, f: (0, 0)),
                  pl.BlockSpec((d, tf), lambda i, f: (0, f)),
                  pl.BlockSpec((d, tf), lambda i, f: (0, f)),
                  pl.BlockSpec((tf, d), lambda i, f: (f, 0))],
        out_specs=pl.BlockSpec((tm, d), lambda i, f: (i, 0)),
        out_shape=jax.ShapeDtypeStruct((t, d), F32),
        scratch_shapes=[pltpu.VMEM((tm, d), BF16)],
        compiler_params=_cparams("arbitrary", "arbitrary"),
        name="ffn_dense",
    )(x, g, wg, wu, wd)


def _route_kernel(x_ref, g_ref, wr_ref, h_ref, gates_ref, idx_ref):
    hf = _rms_rows(x_ref[...], g_ref[...])
    h_ref[...] = hf.astype(BF16)
    logits = _dot_f32(hf, wr_ref[...])
    lane = lax.broadcasted_iota(jnp.int32, logits.shape, 1)
    m1 = jnp.max(logits, axis=-1, keepdims=True)
    i1 = jnp.min(jnp.where(logits == m1, lane, N_EXPERTS), axis=-1, keepdims=True)
    rest = jnp.where(lane == i1, NEG_INF, logits)
    m2 = jnp.max(rest, axis=-1, keepdims=True)
    i2 = jnp.min(jnp.where(rest == m2, lane, N_EXPERTS), axis=-1, keepdims=True)
    e2 = jnp.exp(m2 - m1)
    inv = 1.0 / (1.0 + e2)
    gates_ref[...] = jnp.where(lane == 0, inv, jnp.where(lane == 1, e2 * inv, 0.0))
    idx_ref[...] = jnp.where(lane == 0, i1, jnp.where(lane == 1, i2, 0))


def _moe_route(x, g, wr, tm):
    t, d = x.shape
    ne = wr.shape[1]
    row = lambda i: (i, 0)
    return pl.pallas_call(
        _route_kernel,
        grid=(t // tm,),
        in_specs=[pl.BlockSpec((tm, d), row), pl.BlockSpec((1, d), lambda i: (0, 0)),
                  pl.BlockSpec((d, ne), lambda i: (0, 0))],
        out_specs=[pl.BlockSpec((tm, d), row), pl.BlockSpec((tm, ne), row), pl.BlockSpec((tm, ne), row)],
        out_shape=[jax.ShapeDtypeStruct((t, d), BF16), jax.ShapeDtypeStruct((t, ne), F32),
                   jax.ShapeDtypeStruct((t, ne), jnp.int32)],
        compiler_params=_cparams("arbitrary"),
        name="moe_route",
    )(x, g, wr)


def _expert_kernel(vt_ref, ve_ref, vf_ref, st_ref, en_ref, xs_ref, wg_ref, wu_ref, wd_ref, o_ref, h_ref, *, tm):
    v = pl.program_id(0)
    f = pl.program_id(1)
    lo = jnp.maximum(st_ref[v], vt_ref[v] * tm)
    hi = jnp.minimum(en_ref[v], (vt_ref[v] + 1) * tm)

    @pl.when(hi > lo)
    def _():
        @pl.when(f == 0)
        def _():
            rows = lax.broadcasted_iota(jnp.int32, (tm, 1), 0) + vt_ref[v] * tm
            mine = (rows >= lo) & (rows < hi)
            h_ref[...] = jnp.where(mine, xs_ref[...], jnp.zeros_like(xs_ref))

        part = _swiglu_part(h_ref[...], wg_ref, wu_ref, wd_ref)
        fresh = (vf_ref[v] == 1) & (f == 0)

        @pl.when(fresh)
        def _():
            o_ref[...] = part

        @pl.when(jnp.logical_not(fresh))
        def _():
            o_ref[...] += part


def _moe_experts(xs, sched, wg, wu, wd, tm, tf):
    r, d = xs.shape
    ne, _, fe = wg.shape
    tpe = fe // tf
    n_visits = sched[0].shape[0]
    grid_spec = pltpu.PrefetchScalarGridSpec(
        num_scalar_prefetch=5,
        grid=(n_visits, tpe),
        in_specs=[pl.BlockSpec((tm, d), lambda v, f, vt, ve, vf, st, en: (vt[v], 0)),
                  pl.BlockSpec((None, d, tf), lambda v, f, vt, ve, vf, st, en: (ve[v], 0, f)),
                  pl.BlockSpec((None, d, tf), lambda v, f, vt, ve, vf, st, en: (ve[v], 0, f)),
                  pl.BlockSpec((None, tf, d), lambda v, f, vt, ve, vf, st, en: (ve[v], f, 0))],
        out_specs=pl.BlockSpec((tm, d), lambda v, f, vt, ve, vf, st, en: (vt[v], 0)),
        scratch_shapes=[pltpu.VMEM((tm, d), BF16)])
    return pl.pallas_call(
        functools.partial(_expert_kernel, tm=tm),
        grid_spec=grid_spec,
        out_shape=jax.ShapeDtypeStruct((r, d), F32),
        compiler_params=_cparams("arbitrary", "arbitrary"),
        name="moe_experts",
    )(*sched, xs, wg, wu, wd)


def _combine_kernel(x_ref, y1_ref, y2_ref, gates_ref, o_ref):
    gates = gates_ref[...]
    o_ref[...] = x_ref[...] + (gates[:, 0:1] * y1_ref[...] + gates[:, 1:2] * y2_ref[...])


def _moe_combine(x, y1, y2, gates, tm):
    t, d = x.shape
    row = lambda i: (i, 0)
    wide = pl.BlockSpec((tm, d), row)
    return pl.pallas_call(
        _combine_kernel,
        grid=(t // tm,),
        in_specs=[wide, wide, wide, pl.BlockSpec((tm, gates.shape[1]), row)],
        out_specs=wide,
        out_shape=jax.ShapeDtypeStruct((t, d), F32),
        compiler_params=_cparams("arbitrary"),
        name="moe_combine",
    )(x, y1, y2, gates)


def _expert_schedule(idx, tm):
    t = idx.shape[0]
    i12 = idx[:, :2]
    order = jnp.argsort(i12.reshape(-1), stable=True).astype(jnp.int32)
    src = order // 2
    onehot = (i12[:, :, None] == jnp.arange(N_EXPERTS, dtype=jnp.int32)).sum(axis=1).astype(jnp.int32)
    csum = jnp.cumsum(onehot, axis=0)
    counts = csum[-1]
    ends = jnp.cumsum(counts)
    starts = ends - counts
    pos = jnp.take_along_axis(starts[None, :] + csum - onehot, i12, axis=1)

    n_tiles = (2 * t) // tm
    n_visits = n_tiles + N_EXPERTS - 1
    first_tile = starts // tm
    nvis = jnp.where(counts > 0, (ends - 1) // tm - first_tile + 1, 0)
    vend = jnp.cumsum(nvis)
    v = jnp.arange(n_visits, dtype=jnp.int32)
    active = v < vend[-1]
    ve = jnp.minimum(jnp.sum(v[:, None] >= vend[None, :], axis=1), N_EXPERTS - 1).astype(jnp.int32)
    vt = jnp.where(active, first_tile[ve] + v - (vend - nvis)[ve], n_tiles - 1).astype(jnp.int32)
    vf = jnp.concatenate([jnp.ones((1,), jnp.int32), (vt[1:] != vt[:-1]).astype(jnp.int32)])
    st = jnp.where(active, starts[ve], 0).astype(jnp.int32)
    en = jnp.where(active, ends[ve], 0).astype(jnp.int32)
    return src, pos, (vt, ve, vf, st, en)


def _ffn_moe(x, g, wg, wu, wd, wr, tm, tf):
    t = x.shape[0]
    tme = _pick_tile(2 * t, (512, 256, 128, 64, 8))
    h, gates, idx = _moe_route(x, g, wr, tm)
    src, pos, sched = _expert_schedule(idx, tme)
    def rows(a, ids):
        return a.at[ids].get(mode="promise_in_bounds")

    ys = _moe_experts(rows(h, src), sched, wg, wu, wd, tme, tf)
    return _moe_combine(x, rows(ys, pos[:, 0]), rows(ys, pos[:, 1]), gates, tm)


def _pick_tile(n, prefs):
    for p in prefs:
        if n % p == 0:
            return p
    return n


def _round_up(n, m):
    return (n + m - 1) // m * m


def kernel(x_prompt, x_sample, cache_sb_k, cache_sb_v, cache_fox_k, cache_fox_v, cache_fox_logf, cache_diff_k, cache_diff_v, norm_mix_g, norm_ffn_g, w_in, b_fox_f, fox_q_g, fox_k_g, diff_q_g, diff_k_g, diff_lam, diff_subln_g, w_out, w_ffn_gate, w_ffn_up, w_ffn_down, w_router, w_moe_gate, w_moe_up, w_moe_down):
    bp, tp, d = x_prompt.shape
    nb, ts, _ = x_sample.shape
    depth = w_in.shape[0]
    past = cache_sb_k.shape[2]
    assert bp == 1
    n_s = nb * ts
    t_all = tp + n_s
    tm_in = math.gcd(256, math.gcd(tp, n_s))
    tm_out = math.gcd(512, math.gcd(tp, n_s))
    tm = _pick_tile(t_all, (512, 256, 128, 64, 8))
    def attn_tiles(pref):
        tq_ = _pick_tile(tp, pref[:1] + (1024, 512, 256, 128))
        return tq_, _pick_tile(tq_, pref[1:] + (256, 128))

    tq_sb, tk_sb = attn_tiles(SB_TILE)
    tq_fx, tk_fx = attn_tiles(FOX_TILE)
    tq_df, tk_df = attn_tiles(DIFF_TILE)

    w3 = 3 * GROUP_W
    w_in_r = jnp.concatenate(
        [w_in[:, :, :2 * w3], w_in[:, :, 2 * w3 + N_HEADS:], w_in[:, :, 2 * w3:2 * w3 + N_HEADS],
         jnp.zeros((depth, d, HEAD_DIM - N_HEADS), w_in.dtype)], axis=-1).astype(BF16)
    b_f = jnp.pad(b_fox_f, ((0, 0), (0, HEAD_DIM - N_HEADS)))
    w_out_b = w_out.astype(BF16)
    w_fg, w_fu, w_fd = w_ffn_gate.astype(BF16), w_ffn_up.astype(BF16), w_ffn_down.astype(BF16)
    ff = w_ffn_gate.shape[-1]
    tf = _pick_tile(ff, (512, 256, 128))
    ffe = w_moe_gate.shape[-1]
    tfe = 512 if ffe >= 512 else 128
    pad_e = _round_up(ffe, tfe) - ffe
    w_mg = jnp.pad(w_moe_gate, ((0, 0), (0, 0), (0, 0), (0, pad_e))).astype(BF16)
    w_mu = jnp.pad(w_moe_up, ((0, 0), (0, 0), (0, 0), (0, pad_e))).astype(BF16)
    w_md = jnp.pad(w_moe_down, ((0, 0), (0, 0), (0, pad_e), (0, 0))).astype(BF16)

    slopes = jnp.exp2(-8.0 * jnp.arange(1, N_HEADS + 1, dtype=F32) / N_HEADS)
    slopes = jnp.broadcast_to(slopes[:, None, None], (N_HEADS, 1, HEAD_DIM))

    c_sbk, c_sbv, c_fxk, c_fxv, c_dfk, c_dfv = [
        c.reshape(depth, nb, past * N_HEADS, HEAD_DIM)
        for c in (cache_sb_k, cache_sb_v, cache_fox_k, cache_fox_v, cache_diff_k, cache_diff_v)]

    x = jnp.concatenate([x_prompt.reshape(tp, d), x_sample.reshape(n_s, d)], axis=0)
    stacked_p, stacked_s = (), ()
    logf_p, logf_s = [], []
    for l in range(depth):
        lam_init = 0.8 - 0.6 * math.exp(-0.3 * l)
        dqg = jnp.tile(diff_q_g[l], 2)[None, :]
        dkg = jnp.tile(diff_k_g[l], 2)[None, :]
        proj_args = (norm_mix_g[l][None, :], w_in_r[l], b_f[l][None, :],
                     fox_q_g[l][None, :], fox_k_g[l][None, :], dqg, dkg, tm_in)
        sg = diff_subln_g[l][None, :]
        lam = diff_lam[l]

        res = _inproj(x, 0, tp, l, depth, stacked_p, *proj_args, True)
        sbq, sbk, sbv, fxq, fxk, fxv, dfq, dfk, dfv = res[:9]
        stacked_p = tuple(res[9:9 + N_STACKED])
        logf_p.append(res[15][:, :N_HEADS])
        cum_k = res[16][:, :N_HEADS].T.reshape(N_HEADS, tp // tk_fx, 1, tk_fx)
        o_sb = _sb_prompt(sbq, sbk, sbv, tp, tq_sb, tk_sb)
        o_fx = _fox_prompt(fxq, fxk, fxv, cum_k, tp, tq_fx, tk_fx)
        o_df = _diff_prompt(dfq, dfk, dfv, slopes, lam, sg, tp, tq_df, tk_df, lam_init)
        x = _outproj(x, 0, o_sb, o_fx, o_df, w_out_b[l], tm_out)

        res = _inproj(x, tp // tm_in, n_s, l, depth, stacked_s, *proj_args, False)
        sbq, sbk, sbv, fxq, fxk, fxv, dfq, dfk, dfv = [a.reshape(nb, ts, GROUP_W) for a in res[:9]]
        stacked_s = tuple(res[9:9 + N_STACKED])
        lf_s = res[15][:, :N_HEADS]
        logf_s.append(lf_s)
        lf_new = jnp.pad(jnp.swapaxes(lf_s.reshape(nb, ts, N_HEADS), 1, 2), ((0, 0), (0, 8 - N_HEADS), (0, 0)))
        lf_past = jnp.pad(jnp.swapaxes(cache_fox_logf[l], 1, 2), ((0, 0), (0, 8 - N_HEADS), (0, 0)))
        o_sb = _sb_sample(sbq, sbk, sbv, c_sbk, c_sbv, l)
        o_fx = _fox_sample(fxq, fxk, fxv, c_fxk, c_fxv, lf_past, lf_new, l)
        o_df = _diff_sample(dfq, dfk, dfv, c_dfk, c_dfv, slopes, lam, sg, lam_init, l)
        x = _outproj(x, tp // tm_out, o_sb.reshape(n_s, GROUP_W), o_fx.reshape(n_s, GROUP_W),
                     o_df.reshape(n_s, GROUP_W), w_out_b[l], tm_out)

        if l % 2 == 0:
            x = _ffn_dense(x, norm_ffn_g[l][None, :], w_fg[l // 2], w_fu[l // 2], w_fd[l // 2], tm, tf)
        else:
            x = _ffn_moe(x, norm_ffn_g[l][None, :], w_mg[l // 2], w_mu[l // 2], w_md[l // 2],
                         w_router[l // 2], tm, tfe)

    kv_tail = (N_HEADS, HEAD_DIM)
    sbk_p, sbv_p, fxk_p, fxv_p, dfk_p, dfv_p = [a.reshape((depth, 1, tp) + kv_tail) for a in stacked_p]
    sbk_s, sbv_s, fxk_s, fxv_s, dfk_s, dfv_s = [a.reshape((depth, nb, ts) + kv_tail) for a in stacked_s]
    lf_p = jnp.stack(logf_p).reshape(depth, 1, tp, N_HEADS)
    lf_s = jnp.stack(logf_s).reshape(depth, nb, ts, N_HEADS)
    return (x[:tp].reshape(1, tp, d), x[tp:].reshape(nb, ts, d),
            sbk_p, sbv_p, fxk_p, fxv_p, lf_p, dfk_p, dfv_p,
            sbk_s, sbv_s, fxk_s, fxv_s, lf_s, dfk_s, dfv_s)
```

```python
import functools
import math

import jax
import jax.numpy as jnp
from jax import lax
from jax.experimental import pallas as pl
from jax.experimental.pallas import tpu as pltpu

F32 = jnp.float32
BF16 = jnp.bfloat16

LANES = 128
HEAD_DIM = 128
N_HEADS = 4
GROUP_W = N_HEADS * HEAD_DIM
DIFF_QK = HEAD_DIM // 2
CHUNK = 64
CHUNK_SHIFT = 6
N_EXPERTS = 8
RMS_EPS = 1e-6
LOG2E = math.log2(math.e)
VMEM_LIMIT = 56 * 1024 * 1024
SB_TILE = (2048, 256)
FOX_TILE = (2048, 1024)
DIFF_TILE = (2048, 512)
NEG_INF = float("-inf")
SIGN_BIT = -2 ** 31
HIGHEST = lax.Precision.HIGHEST


def _cparams(*sem):
    return pltpu.CompilerParams(dimension_semantics=sem, vmem_limit_bytes=VMEM_LIMIT)


def _vmem_whole():
    return pl.BlockSpec(memory_space=pltpu.VMEM)


def _any_space():
    return pl.BlockSpec(memory_space=pl.ANY)


def _rms_rows(x, g):
    return x * lax.rsqrt(jnp.mean(x * x, axis=-1, keepdims=True) + RMS_EPS) * g


def _log_sigmoid(x):
    return -(jnp.maximum(-x, 0.0) + jnp.log1p(jnp.exp(-jnp.abs(x))))


def _softplus2(z):
    neg_abs = lax.bitcast_convert_type(lax.bitcast_convert_type(z, jnp.int32) | SIGN_BIT, F32)
    return jnp.maximum(z, 0.0) + jnp.log2(1.0 + jnp.exp2(neg_abs))


def _dot(a, b):
    return jnp.dot(a, b, preferred_element_type=F32)


def _dot_f32(a, b):
    return jnp.dot(a, b, preferred_element_type=F32, precision=HIGHEST)


def _dot_nt(a, b):
    return lax.dot_general(a, b, (((1,), (1,)), ((), ())), preferred_element_type=F32)


def _lane_blocks(x):
    return [x[:, c * LANES:(c + 1) * LANES] for c in range(x.shape[1] // LANES)]


def _sum_list(xs):
    acc = xs[0]
    for x in xs[1:]:
        acc = acc + x
    return acc


def _max_list(xs):
    acc = xs[0]
    for x in xs[1:]:
        acc = jnp.maximum(acc, x)
    return acc


def _head_rms(z, g, group):
    outs = []
    for hh in range(N_HEADS):
        zz = z[:, hh * HEAD_DIM:(hh + 1) * HEAD_DIM]
        sq = zz * zz
        if group == HEAD_DIM:
            r = lax.rsqrt(jnp.mean(sq, axis=-1, keepdims=True) + RMS_EPS)
        else:
            lo = lax.broadcasted_iota(jnp.int32, zz.shape, 1) < DIFF_QK
            s_lo = jnp.sum(jnp.where(lo, sq, 0.0), axis=-1, keepdims=True)
            s_hi = jnp.sum(jnp.where(lo, 0.0, sq), axis=-1, keepdims=True)
            r = jnp.where(lo, lax.rsqrt(s_lo / DIFF_QK + RMS_EPS), lax.rsqrt(s_hi / DIFF_QK + RMS_EPS))
        outs.append(zz * r * g)
    return jnp.concatenate(outs, axis=-1)


N_STACKED = 6


def _store_heads(o_ref, z):
    rows = z.shape[0]
    for hh in range(N_HEADS):
        o_ref[pl.ds(hh, rows, stride=N_HEADS), :] = z[:, hh * HEAD_DIM:(hh + 1) * HEAD_DIM]


def _inproj_kernel(*refs, with_cumsum, aliased):
    x_ref, g_ref, w_ref, bf_ref, fqg_ref, fkg_ref, dqg_ref, dkg_ref, tri_ref = refs[:9]
    outs = refs[9 + (N_STACKED if aliased else 0):]
    (sbq, sbk, sbv, fxq, fxk, fxv, dfq, dfk, dfv,
     sbk32, sbv32, fxk32, fxv32, dfk32, dfv32, logf, cumf) = outs[:17]
    h = _rms_rows(x_ref[...], g_ref[...]).astype(BF16)

    def grp(c):
        return _dot(h, w_ref[:, c * GROUP_W:(c + 1) * GROUP_W])

    z = grp(0)
    sbq[...] = (z * (HEAD_DIM ** -0.5 * LOG2E)).astype(BF16)
    z = grp(1)
    _store_heads(sbk32, z)
    sbk[...] = z.astype(BF16)
    z = grp(2)
    _store_heads(sbv32, z)
    sbv[...] = z.astype(BF16)

    z = _head_rms(grp(3), fqg_ref[...], HEAD_DIM)
    fxq[...] = (z * (HEAD_DIM ** -0.5 * LOG2E)).astype(BF16)
    z = _head_rms(grp(4), fkg_ref[...], HEAD_DIM)
    _store_heads(fxk32, z)
    fxk[...] = z.astype(BF16)
    z = grp(5)
    _store_heads(fxv32, z)
    fxv[...] = z.astype(BF16)

    z = _head_rms(grp(6), dqg_ref[...], DIFF_QK)
    dfq[...] = (z * (DIFF_QK ** -0.5 * LOG2E)).astype(BF16)
    z = _head_rms(grp(7), dkg_ref[...], DIFF_QK)
    _store_heads(dfk32, z)
    dfk[...] = z.astype(BF16)
    z = grp(8)
    _store_heads(dfv32, z)
    dfv[...] = z.astype(BF16)

    zf = _dot(h, w_ref[:, 9 * GROUP_W:9 * GROUP_W + HEAD_DIM])
    lf = _log_sigmoid(zf + bf_ref[...])
    logf[...] = lf
    if with_cumsum:
        carry_ref = outs[17]

        @pl.when(pl.program_id(0) == 0)
        def _():
            carry_ref[...] = jnp.zeros_like(carry_ref)

        cum = _dot_f32(tri_ref[...], lf) + carry_ref[0:1, :]
        cumf[...] = cum
        tm = cum.shape[0]
        carry_ref[...] = jnp.broadcast_to(cum[tm - 1:tm, :], carry_ref.shape)
    else:
        cumf[...] = lf


def _inproj(x, tile0, rows, layer, depth, stacked, g, w, bf, fqg, fkg, dqg, dkg, tm, with_cumsum):
    d = x.shape[1]
    aliased = layer > 0
    row = lambda i: (i, 0)
    full = lambda i: (0, 0)
    small = pl.BlockSpec((1, HEAD_DIM), full)
    slab = pl.BlockSpec((tm, GROUP_W), row)
    narrow = pl.BlockSpec((tm, HEAD_DIM), row)
    stacked_spec = pl.BlockSpec((None, tm * N_HEADS, HEAD_DIM), lambda i: (layer, i, 0))
    tri = jnp.tri(tm, dtype=F32)
    in_specs = [pl.BlockSpec((tm, d), lambda i: (i + tile0, 0)), pl.BlockSpec((1, d), full), _vmem_whole(),
                small, small, small, small, small, pl.BlockSpec((tm, tm), full)]
    args = [x, g, w, bf, fqg, fkg, dqg, dkg, tri]
    aliases = {}
    if aliased:
        in_specs += [_any_space()] * N_STACKED
        aliases = {len(args) + k: 9 + k for k in range(N_STACKED)}
        args += list(stacked)
    out_bf = jax.ShapeDtypeStruct((rows, GROUP_W), BF16)
    out_n = jax.ShapeDtypeStruct((rows, HEAD_DIM), F32)
    out_st = jax.ShapeDtypeStruct((depth, rows * N_HEADS, HEAD_DIM), F32)
    return pl.pallas_call(
        functools.partial(_inproj_kernel, with_cumsum=with_cumsum, aliased=aliased),
        grid=(rows // tm,),
        in_specs=in_specs,
        out_specs=[slab] * 9 + [stacked_spec] * N_STACKED + [narrow, narrow],
        out_shape=[out_bf] * 9 + [out_st] * N_STACKED + [out_n, out_n],
        scratch_shapes=[pltpu.VMEM((8, HEAD_DIM), F32)] if with_cumsum else [],
        input_output_aliases=aliases,
        compiler_params=_cparams("arbitrary"),
        name="inproj",
    )(*args)


def _sb_prompt_kernel(q_ref, k_ref, v_ref, tri_ref, o_ref, c_ref, acc_ref, *, tq, tk):
    i = pl.program_id(1)
    ratio = tq // tk
    row = lax.broadcasted_iota(jnp.int32, (tk, tk), 0)
    col = lax.broadcasted_iota(jnp.int32, (tk, tk), 1)
    strictly_before = col < row
    c_ref[...] = jnp.zeros_like(c_ref)
    acc_ref[...] = jnp.zeros_like(acc_ref)

    def step(j, r0, r1, masked):
        kc, vc = _kv_chunk(k_ref, v_ref, j, tk)
        z = _dot_nt(q_ref[r0:r1, :], kc)
        cost = _softplus2(z)
        if masked:
            cost = jnp.where(strictly_before, cost, 0.0)
        incl = _dot(cost.astype(BF16), tri_ref[...])
        w = jnp.exp2(z - incl)
        if masked:
            w = jnp.where(strictly_before, w, 0.0)
        carried = c_ref[r0:r1, :]
        acc_ref[r0:r1, :] += jnp.exp2(-carried) * _dot(w.astype(BF16), vc)
        c_ref[r0:r1, :] = carried + incl[:, 0:1]

    for d in reversed(range(ratio)):
        step(i * ratio + d, d * tk, (d + 1) * tk, True)
        if d + 1 < ratio:
            step(i * ratio + d, (d + 1) * tk, tq, False)

    def body(jj, carry):
        step(i * ratio - 1 - jj, 0, tq, False)
        return carry

    lax.fori_loop(0, i * ratio, body, 0)
    o_ref[...] = acc_ref[...].astype(BF16)


def _sb_prompt(q, k, v, tp, tq, tk):
    tri = jnp.tri(tk, dtype=BF16)
    return pl.pallas_call(
        functools.partial(_sb_prompt_kernel, tq=tq, tk=tk),
        grid=(N_HEADS, tp // tq),
        in_specs=[pl.BlockSpec((tq, HEAD_DIM), lambda h, i: (i, h)),
                  _kv_spec(tp), _kv_spec(tp),
                  pl.BlockSpec((tk, tk), lambda h, i: (0, 0))],
        out_specs=pl.BlockSpec((tq, HEAD_DIM), lambda h, i: (i, h)),
        out_shape=jax.ShapeDtypeStruct((tp, GROUP_W), BF16),
        scratch_shapes=[pltpu.VMEM((tq, LANES), F32), pltpu.VMEM((tq, HEAD_DIM), F32)],
        compiler_params=_cparams("arbitrary", "arbitrary"),
        name="sb_prompt",
    )(q, k, v, tri)


def _softmax_step(s, vc, m_ref, l_ref, acc_ref, r0, r1):
    blocks = _lane_blocks(s)
    m_prev = m_ref[r0:r1, :]
    m_new = jnp.maximum(m_prev, jnp.max(_max_list(blocks), axis=-1, keepdims=True))
    alpha = jnp.exp2(m_prev - m_new)
    ps = [jnp.exp2(b - m_new) for b in blocks]
    l_ref[r0:r1, :] = alpha * l_ref[r0:r1, :] + _sum_list(ps)
    acc_ref[r0:r1, :] = alpha * acc_ref[r0:r1, :] + _dot(jnp.concatenate(ps, axis=1).astype(BF16), vc)
    m_ref[r0:r1, :] = m_new


def _softmax_init(m_ref, l_ref, acc_ref):
    m_ref[...] = jnp.full_like(m_ref, NEG_INF)
    l_ref[...] = jnp.zeros_like(l_ref)
    acc_ref[...] = jnp.zeros_like(acc_ref)


def _softmax_result(l_ref, acc_ref):
    return acc_ref[...] * (1.0 / jnp.sum(l_ref[...], axis=-1, keepdims=True))


def _diagonal_steps(step, n_full, ratio, tq, tk):
    for d in range(ratio):
        step(n_full + d, d * tk, (d + 1) * tk, True)
        if d + 1 < ratio:
            step(n_full + d, (d + 1) * tk, tq, False)


def _kv_chunk(k_ref, v_ref, j, tk):
    start = pl.multiple_of(j * tk, tk)
    return k_ref[pl.ds(start, tk), :], v_ref[pl.ds(start, tk), :]


def _fox_prompt_kernel(q_ref, k_ref, v_ref, fk_ref, o_ref, m_ref, l_ref, acc_ref, *, tq, tk):
    i = pl.program_id(1)
    ratio = tq // tk
    n_full = i * ratio
    f0 = fk_ref[n_full][:, 0:1]
    _softmax_init(m_ref, l_ref, acc_ref)
    row = lax.broadcasted_iota(jnp.int32, (tk, tk), 0)
    col = lax.broadcasted_iota(jnp.int32, (tk, tk), 1)

    def step(j, r0, r1, masked):
        kc, vc = _kv_chunk(k_ref, v_ref, j, tk)
        s = _dot_nt(q_ref[r0:r1, :], kc) + (f0 - fk_ref[j]) * LOG2E
        if masked:
            s = jnp.where(col <= row, s, NEG_INF)
        _softmax_step(s, vc, m_ref, l_ref, acc_ref, r0, r1)

    _diagonal_steps(step, n_full, ratio, tq, tk)

    def body(j, carry):
        step(j, 0, tq, False)
        return carry

    lax.fori_loop(0, n_full, body, 0)
    o_ref[...] = _softmax_result(l_ref, acc_ref).astype(BF16)


def _kv_spec(tp):
    return pl.BlockSpec((tp, HEAD_DIM), lambda h, i: (0, h), pipeline_mode=pl.Buffered(1))


def _fox_prompt(q, k, v, fk, tp, tq, tk):
    stat = lambda: pltpu.VMEM((tq, LANES), F32)
    return pl.pallas_call(
        functools.partial(_fox_prompt_kernel, tq=tq, tk=tk),
        grid=(N_HEADS, tp // tq),
        in_specs=[pl.BlockSpec((tq, HEAD_DIM), lambda h, i: (i, h)),
                  _kv_spec(tp), _kv_spec(tp),
                  pl.BlockSpec((None, tp // tk, 1, tk), lambda h, i: (h, 0, 0, 0))],
        out_specs=pl.BlockSpec((tq, HEAD_DIM), lambda h, i: (i, h)),
        out_shape=jax.ShapeDtypeStruct((tp, GROUP_W), BF16),
        scratch_shapes=[stat(), stat(), pltpu.VMEM((tq, HEAD_DIM), F32)],
        compiler_params=_cparams("arbitrary", "arbitrary"),
        name="fox_prompt",
    )(q, k, v, fk)


def _diff_lambda(lam_ref, lam_init):
    lv = lam_ref[...]
    a = jnp.sum(lv[0:1, :] * lv[1:2, :], axis=-1, keepdims=True)
    b = jnp.sum(lv[2:3, :] * lv[3:4, :], axis=-1, keepdims=True)
    return jnp.exp(a) - jnp.exp(b) + lam_init


def _split_halves(q):
    first = lax.broadcasted_iota(jnp.int32, q.shape, 1) < DIFF_QK
    zero = jnp.zeros_like(q)
    return jnp.where(first, q, zero), jnp.where(first, zero, q)


def _diff_prompt_kernel(q_ref, k_ref, v_ref, slope_ref, lam_ref, sg_ref, o_ref,
                        m1_ref, l1_ref, a1_ref, m2_ref, l2_ref, a2_ref, *, tq, tk, lam_init):
    i = pl.program_id(1)
    ratio = tq // tk
    n_full = i * ratio
    slope = slope_ref[:, 0:1] * LOG2E
    _softmax_init(m1_ref, l1_ref, a1_ref)
    _softmax_init(m2_ref, l2_ref, a2_ref)
    row = lax.broadcasted_iota(jnp.int32, (tk, tk), 0)
    col = lax.broadcasted_iota(jnp.int32, (tk, tk), 1)
    visible = lax.shift_right_logical(col, CHUNK_SHIFT) <= lax.shift_right_logical(row, CHUNK_SHIFT)
    own_rel = row - jnp.abs(row - col)
    col_row = lax.broadcasted_iota(jnp.int32, (1, tk), 1)
    q1_all, q2_all = _split_halves(q_ref[...])

    def step(j, r0, r1, masked):
        kc, vc = _kv_chunk(k_ref, v_ref, j, tk)
        q1, q2 = q1_all[r0:r1, :], q2_all[r0:r1, :]
        first_key = j * tk - i * tq
        if masked:
            bias = jnp.where(visible, slope * (own_rel + first_key).astype(F32), NEG_INF)
        else:
            bias = slope * (col_row + first_key).astype(F32)
        _softmax_step(_dot_nt(q1, kc) + bias, vc, m1_ref, l1_ref, a1_ref, r0, r1)
        _softmax_step(_dot_nt(q2, kc) + bias, vc, m2_ref, l2_ref, a2_ref, r0, r1)

    _diagonal_steps(step, n_full, ratio, tq, tk)

    def body(j, carry):
        step(j, 0, tq, False)
        return carry

    lax.fori_loop(0, n_full, body, 0)
    lam = _diff_lambda(lam_ref, lam_init)
    o = _softmax_result(l1_ref, a1_ref) - lam * _softmax_result(l2_ref, a2_ref)
    o_ref[...] = (_rms_rows(o, sg_ref[...]) * (1.0 - lam_init)).astype(BF16)


def _diff_prompt(q, k, v, slopes, lam, subln_g, tp, tq, tk, lam_init):
    stat = lambda: pltpu.VMEM((tq, LANES), F32)
    acc = lambda: pltpu.VMEM((tq, HEAD_DIM), F32)
    return pl.pallas_call(
        functools.partial(_diff_prompt_kernel, tq=tq, tk=tk, lam_init=lam_init),
        grid=(N_HEADS, tp // tq),
        in_specs=[pl.BlockSpec((tq, HEAD_DIM), lambda h, i: (i, h)),
                  _kv_spec(tp), _kv_spec(tp),
                  pl.BlockSpec((None, 1, HEAD_DIM), lambda h, i: (h, 0, 0)),
                  pl.BlockSpec((4, DIFF_QK), lambda h, i: (0, 0)),
                  pl.BlockSpec((1, HEAD_DIM), lambda h, i: (0, 0))],
        out_specs=pl.BlockSpec((tq, HEAD_DIM), lambda h, i: (i, h)),
        out_shape=jax.ShapeDtypeStruct((tp, GROUP_W), BF16),
        scratch_shapes=[stat(), stat(), acc(), stat(), stat(), acc()],
        compiler_params=_cparams("arbitrary", "arbitrary"),
        name="diff_prompt",
    )(q, k, v, slopes, lam, subln_g)


def _split_bf16(x):
    hi = x.astype(BF16)
    return hi, (x - hi.astype(F32)).astype(BF16)


def _sb_sample_kernel(q_ref, kn_ref, vn_ref, ck_ref, cv_ref, trip_ref, trin_ref, o_ref, *, past, t):
    rown = lax.broadcasted_iota(jnp.int32, (t, t), 0)
    coln = lax.broadcasted_iota(jnp.int32, (t, t), 1)
    allowed_n = coln < rown
    heads = range(N_HEADS)
    sls = [slice(hh * HEAD_DIM, (hh + 1) * HEAD_DIM) for hh in heads]
    zps = [_dot_nt(q_ref[:, sls[hh]], _cache_head(ck_ref, hh)) for hh in heads]
    zns = [_dot_nt(q_ref[:, sls[hh]], kn_ref[:, sls[hh]]) for hh in heads]
    cps = [_softplus2(zp) for zp in zps]
    cns = [jnp.where(allowed_n, _softplus2(zn), 0.0) for zn in zns]

    def suffix_sums(costs, tri_ref):
        hi, lo = _split_bf16(jnp.concatenate(costs, axis=0))
        both = _dot(jnp.concatenate([hi, lo], axis=0), tri_ref[...])
        return both[:N_HEADS * t, :] + both[N_HEADS * t:, :]

    later_p = suffix_sums(cps, trip_ref)
    later_n = suffix_sums(cns, trin_ref)
    outs = []
    for hh in heads:
        rows = slice(hh * t, (hh + 1) * t)
        after_p = later_p[rows, :] + jnp.sum(cns[hh], axis=-1, keepdims=True)
        wp = jnp.exp2(zps[hh] - cps[hh] - after_p)
        wn = jnp.where(allowed_n, jnp.exp2(zns[hh] - cns[hh] - later_n[rows, :]), 0.0)
        outs.append(_dot(wp.astype(BF16), _cache_head(cv_ref, hh)) + _dot(wn.astype(BF16), vn_ref[:, sls[hh]]))
    o_ref[...] = jnp.concatenate(outs, axis=-1).astype(BF16)


def _cache_head(c_ref, hh):
    past = c_ref.shape[0] // N_HEADS
    return c_ref[pl.ds(hh, past, stride=N_HEADS), :].astype(BF16)


def _sample_specs(t, past, layer):
    new = pl.BlockSpec((None, t, GROUP_W), lambda b: (b, 0, 0))
    cache = pl.BlockSpec((None, None, past * N_HEADS, HEAD_DIM), lambda b: (layer, b, 0, 0))
    return new, cache


def _sb_sample(q, kn, vn, ck, cv, layer):
    nb, t, _ = q.shape
    past = ck.shape[2] // N_HEADS
    new, cache = _sample_specs(t, past, layer)
    trip = jnp.tri(past, k=-1, dtype=BF16)
    trin = jnp.tri(t, k=-1, dtype=BF16)
    return pl.pallas_call(
        functools.partial(_sb_sample_kernel, past=past, t=t),
        grid=(nb,),
        in_specs=[new, new, new, cache, cache,
                  pl.BlockSpec((past, past), lambda b: (0, 0)),
                  pl.BlockSpec((t, t), lambda b: (0, 0))],
        out_specs=new,
        out_shape=jax.ShapeDtypeStruct((nb, t, GROUP_W), BF16),
        compiler_params=_cparams("arbitrary"),
        name="sb_sample",
    )(q, kn, vn, ck, cv, trip, trin)


def _cumsum_kernel(x_ref, tri_ref, o_ref):
    o_ref[...] = _dot_f32(x_ref[...], tri_ref[...])


def _cumsum_lanes(x):
    rows, n = x.shape
    tri = jnp.tri(n, dtype=F32).T
    return pl.pallas_call(
        _cumsum_kernel,
        in_specs=[_vmem_whole(), _vmem_whole()],
        out_specs=_vmem_whole(),
        out_shape=jax.ShapeDtypeStruct((rows, n), F32),
        compiler_params=pltpu.CompilerParams(vmem_limit_bytes=VMEM_LIMIT),
        name="cache_logf_cumsum",
    )(x, tri)


def _fox_sample_kernel(q_ref, kn_ref, vn_ref, ck_ref, cv_ref, cump_ref, lfn_ref, trin_ref, o_ref, *, past, t):
    rown = lax.broadcasted_iota(jnp.int32, (t, t), 0)
    coln = lax.broadcasted_iota(jnp.int32, (t, t), 1)
    cum_p = cump_ref[...]
    total = cum_p[:, past - 1:past]
    bias_p = (total - cum_p) * LOG2E
    bias_n = -_dot_f32(lfn_ref[...], trin_ref[...]) * LOG2E
    outs = []
    for hh in range(N_HEADS):
        sl = slice(hh * HEAD_DIM, (hh + 1) * HEAD_DIM)
        q = q_ref[:, sl]
        sp = _dot_nt(q, _cache_head(ck_ref, hh)) + bias_p[hh:hh + 1, :]
        sn = _dot_nt(q, kn_ref[:, sl]) + bias_n[hh:hh + 1, :]
        sn = jnp.where(coln <= rown, sn, NEG_INF)
        m = jnp.maximum(jnp.max(sp, axis=-1, keepdims=True), jnp.max(sn, axis=-1, keepdims=True))
        pp = jnp.exp2(sp - m)
        pn = jnp.exp2(sn - m)
        inv = 1.0 / (jnp.sum(pp, axis=-1, keepdims=True) + jnp.sum(pn, axis=-1, keepdims=True))
        outs.append(_dot((pp * inv).astype(BF16), _cache_head(cv_ref, hh))
                    + _dot((pn * inv).astype(BF16), vn_ref[:, sl]))
    o_ref[...] = jnp.concatenate(outs, axis=-1).astype(BF16)


def _fox_sample(q, kn, vn, ck, cv, cum_past, lf_new, layer):
    nb, t, _ = q.shape
    past = ck.shape[2] // N_HEADS
    new, cache = _sample_specs(t, past, layer)
    trin = jnp.tri(t, dtype=F32).T
    return pl.pallas_call(
        functools.partial(_fox_sample_kernel, past=past, t=t),
        grid=(nb,),
        in_specs=[new, new, new, cache, cache,
                  pl.BlockSpec((None, None, 8, past), lambda b: (layer, b, 0, 0)),
                  pl.BlockSpec((None, 8, t), lambda b: (b, 0, 0)),
                  pl.BlockSpec((t, t), lambda b: (0, 0))],
        out_specs=new,
        out_shape=jax.ShapeDtypeStruct((nb, t, GROUP_W), BF16),
        compiler_params=_cparams("arbitrary"),
        name="fox_sample",
    )(q, kn, vn, ck, cv, cum_past, lf_new, trin)


def _diff_sample_kernel(q_ref, kn_ref, vn_ref, ck_ref, cv_ref, slope_ref, lam_ref, sg_ref, o_ref,
                        *, past, t, lam_init):
    lam = _diff_lambda(lam_ref, lam_init)
    q_pos_p = lax.broadcasted_iota(jnp.int32, (t, past), 0) + past
    k_pos_p = lax.broadcasted_iota(jnp.int32, (t, past), 1)
    q_pos_n = lax.broadcasted_iota(jnp.int32, (t, t), 0) + past
    k_pos_n = lax.broadcasted_iota(jnp.int32, (t, t), 1) + past
    vis_p = lax.shift_right_logical(k_pos_p, CHUNK_SHIFT) <= lax.shift_right_logical(q_pos_p, CHUNK_SHIFT)
    vis_n = lax.shift_right_logical(k_pos_n, CHUNK_SHIFT) <= lax.shift_right_logical(q_pos_n, CHUNK_SHIFT)
    dist_p = jnp.abs(q_pos_p - k_pos_p).astype(F32)
    dist_n = jnp.abs(q_pos_n - k_pos_n).astype(F32)
    outs = []
    for hh in range(N_HEADS):
        sl = slice(hh * HEAD_DIM, (hh + 1) * HEAD_DIM)
        slope = slope_ref[hh][:, 0:1] * LOG2E
        bias_p = jnp.where(vis_p, -slope * dist_p, NEG_INF)
        bias_n = jnp.where(vis_n, -slope * dist_n, NEG_INF)
        kp = _cache_head(ck_ref, hh)
        kn = kn_ref[:, sl]
        p_p, p_n = None, None
        for qh, coef in zip(_split_halves(q_ref[:, sl]), (None, lam)):
            sp = _dot_nt(qh, kp) + bias_p
            sn = _dot_nt(qh, kn) + bias_n
            m = jnp.maximum(jnp.max(sp, axis=-1, keepdims=True), jnp.max(sn, axis=-1, keepdims=True))
            ep = jnp.exp2(sp - m)
            en = jnp.exp2(sn - m)
            inv = 1.0 / (jnp.sum(ep, axis=-1, keepdims=True) + jnp.sum(en, axis=-1, keepdims=True))
            if coef is None:
                p_p, p_n = ep * inv, en * inv
            else:
                p_p, p_n = p_p - coef * (ep * inv), p_n - coef * (en * inv)
        o = _dot(p_p.astype(BF16), _cache_head(cv_ref, hh)) + _dot(p_n.astype(BF16), vn_ref[:, sl])
        outs.append(_rms_rows(o, sg_ref[...]) * (1.0 - lam_init))
    o_ref[...] = jnp.concatenate(outs, axis=-1).astype(BF16)


def _diff_sample(q, kn, vn, ck, cv, slopes, lam, subln_g, lam_init, layer):
    nb, t, _ = q.shape
    past = ck.shape[2] // N_HEADS
    new, cache = _sample_specs(t, past, layer)
    return pl.pallas_call(
        functools.partial(_diff_sample_kernel, past=past, t=t, lam_init=lam_init),
        grid=(nb,),
        in_specs=[new, new, new, cache, cache,
                  pl.BlockSpec((N_HEADS, 1, HEAD_DIM), lambda b: (0, 0, 0)),
                  pl.BlockSpec((4, DIFF_QK), lambda b: (0, 0)),
                  pl.BlockSpec((1, HEAD_DIM), lambda b: (0, 0))],
        out_specs=new,
        out_shape=jax.ShapeDtypeStruct((nb, t, GROUP_W), BF16),
        compiler_params=_cparams("arbitrary"),
        name="diff_sample",
    )(q, kn, vn, ck, cv, slopes, lam, subln_g)


def _outproj_kernel(x_ref, osb_ref, ofx_ref, odf_ref, w_ref, *rest):
    o_ref = rest[-1]
    y = _dot(osb_ref[...], w_ref[0:GROUP_W, :])
    y += _dot(ofx_ref[...], w_ref[GROUP_W:2 * GROUP_W, :])
    y += _dot(odf_ref[...], w_ref[2 * GROUP_W:3 * GROUP_W, :])
    o_ref[...] = x_ref[...] + y


def _outproj(x, tile0, o_sb, o_fox, o_diff, w, tm):
    rows = o_sb.shape[0]
    d = x.shape[1]
    row = lambda i: (i, 0)
    xrow = lambda i: (i + tile0, 0)
    slab = pl.BlockSpec((tm, GROUP_W), row)
    return pl.pallas_call(
        _outproj_kernel,
        grid=(rows // tm,),
        in_specs=[pl.BlockSpec((tm, d), xrow), slab, slab, slab, _vmem_whole()],
        out_specs=pl.BlockSpec((tm, d), xrow),
        out_shape=jax.ShapeDtypeStruct(x.shape, F32),
        input_output_aliases={0: 0},
        compiler_params=_cparams("arbitrary"),
        name="outproj",
    )(x, o_sb, o_fox, o_diff, w)


def _outproj_into(x_rows, stream, tile0, total_rows, o_sb, o_fox, o_diff, w, tm):
    rows, d = x_rows.shape
    row = lambda i: (i, 0)
    slab = pl.BlockSpec((tm, GROUP_W), row)
    in_specs = [pl.BlockSpec((tm, d), row), slab, slab, slab, _vmem_whole()]
    args = [x_rows, o_sb, o_fox, o_diff, w]
    aliases = {}
    if stream is not None:
        in_specs.append(_any_space())
        aliases = {len(args): 0}
        args.append(stream)
    return pl.pallas_call(
        _outproj_kernel,
        grid=(rows // tm,),
        in_specs=in_specs,
        out_specs=pl.BlockSpec((tm, d), lambda i: (i + tile0, 0)),
        out_shape=jax.ShapeDtypeStruct((total_rows, d), F32),
        input_output_aliases=aliases,
        compiler_params=_cparams("arbitrary"),
        name="outproj",
    )(*args)


def _swiglu_part(h, wg_ref, wu_ref, wd_ref):
    gate = _dot(h, wg_ref[...])
    up = _dot(h, wu_ref[...])
    a = gate * jax.nn.sigmoid(gate) * up
    return _dot(a.astype(BF16), wd_ref[...])


def _ffn_kernel(x_ref, g_ref, wg_ref, wu_ref, wd_ref, o_ref, h_ref):
    f = pl.program_id(1)

    @pl.when(f == 0)
    def _():
        h_ref[...] = _rms_rows(x_ref[...], g_ref[...]).astype(BF16)

    part = _swiglu_part(h_ref[...], wg_ref, wu_ref, wd_ref)

    @pl.when(f == 0)
    def _():
        o_ref[...] = x_ref[...] + part

    @pl.when(f != 0)
    def _():
        o_ref[...] += part


def _ffn_dense(x, g, wg, wu, wd, tm, tf):
    t, d = x.shape
    nf = wg.shape[1] // tf
    return pl.pallas_call(
        _ffn_kernel,
        grid=(t // tm, nf),
        in_specs=[pl.BlockSpec((tm, d), lambda i, f: (i, 0)),
                  pl.BlockSpec((1, d), lambda i, f: (0, 0)),
                  pl.BlockSpec((d, tf), lambda i, f: (0, f)),
                  pl.BlockSpec((d, tf), lambda i, f: (0, f)),
                  pl.BlockSpec((tf, d), lambda i, f: (f, 0))],
        out_specs=pl.BlockSpec((tm, d), lambda i, f: (i, 0)),
        out_shape=jax.ShapeDtypeStruct((t, d), F32),
        scratch_shapes=[pltpu.VMEM((tm, d), BF16)],
        compiler_params=_cparams("arbitrary", "arbitrary"),
        name="ffn_dense",
    )(x, g, wg, wu, wd)


def _route_kernel(x_ref, g_ref, wr_ref, h_ref, gates_ref, idx_ref):
    hf = _rms_rows(x_ref[...], g_ref[...])
    h_ref[...] = hf.astype(BF16)
    logits = _dot_f32(hf, wr_ref[...])
    lane = lax.broadcasted_iota(jnp.int32, logits.shape, 1)
    m1 = jnp.max(logits, axis=-1, keepdims=True)
    i1 = jnp.min(jnp.where(logits == m1, lane, N_EXPERTS), axis=-1, keepdims=True)
    rest = jnp.where(lane == i1, NEG_INF, logits)
    m2 = jnp.max(rest, axis=-1, keepdims=True)
    i2 = jnp.min(jnp.where(rest == m2, lane, N_EXPERTS), axis=-1, keepdims=True)
    e2 = jnp.exp(m2 - m1)
    inv = 1.0 / (1.0 + e2)
    gates_ref[...] = jnp.where(lane == 0, inv, jnp.where(lane == 1, e2 * inv, 0.0))
    idx_ref[...] = jnp.where(lane == 0, i1, jnp.where(lane == 1, i2, 0))


def _moe_route(x, g, wr, tm):
    t, d = x.shape
    ne = wr.shape[1]
    row = lambda i: (i, 0)
    return pl.pallas_call(
        _route_kernel,
        grid=(t // tm,),
        in_specs=[pl.BlockSpec((tm, d), row), pl.BlockSpec((1, d), lambda i: (0, 0)),
                  pl.BlockSpec((d, ne), lambda i: (0, 0))],
        out_specs=[pl.BlockSpec((tm, d), row), pl.BlockSpec((tm, ne), row), pl.BlockSpec((tm, ne), row)],
        out_shape=[jax.ShapeDtypeStruct((t, d), BF16), jax.ShapeDtypeStruct((t, ne), F32),
                   jax.ShapeDtypeStruct((t, ne), jnp.int32)],
        compiler_params=_cparams("arbitrary"),
        name="moe_route",
    )(x, g, wr)


def _expert_kernel(vt_ref, ve_ref, vf_ref, st_ref, en_ref, xs_ref, wg_ref, wu_ref, wd_ref, o_ref, h_ref, *, tm):
    v = pl.program_id(0)
    f = pl.program_id(1)
    lo = jnp.maximum(st_ref[v], vt_ref[v] * tm)
    hi = jnp.minimum(en_ref[v], (vt_ref[v] + 1) * tm)

    @pl.when(hi > lo)
    def _():
        @pl.when(f == 0)
        def _():
            rows = lax.broadcasted_iota(jnp.int32, (tm, 1), 0) + vt_ref[v] * tm
            mine = (rows >= lo) & (rows < hi)
            h_ref[...] = jnp.where(mine, xs_ref[...], jnp.zeros_like(xs_ref))

        part = _swiglu_part(h_ref[...], wg_ref, wu_ref, wd_ref)
        fresh = (vf_ref[v] == 1) & (f == 0)

        @pl.when(fresh)
        def _():
            o_ref[...] = part

        @pl.when(jnp.logical_not(fresh))
        def _():
            o_ref[...] += part


def _moe_experts(xs, sched, wg, wu, wd, tm, tf):
    r, d = xs.shape
    ne, _, fe = wg.shape
    tpe = fe // tf
    n_visits = sched[0].shape[0]
    grid_spec = pltpu.PrefetchScalarGridSpec(
        num_scalar_prefetch=5,
        grid=(n_visits, tpe),
        in_specs=[pl.BlockSpec((tm, d), lambda v, f, vt, ve, vf, st, en: (vt[v], 0)),
                  pl.BlockSpec((None, d, tf), lambda v, f, vt, ve, vf, st, en: (ve[v], 0, f)),
                  pl.BlockSpec((None, d, tf), lambda v, f, vt, ve, vf, st, en: (ve[v], 0, f)),
                  pl.BlockSpec((None, tf, d), lambda v, f, vt, ve, vf, st, en: (ve[v], f, 0))],
        out_specs=pl.BlockSpec((tm, d), lambda v, f, vt, ve, vf, st, en: (vt[v], 0)),
        scratch_shapes=[pltpu.VMEM((tm, d), BF16)])
    return pl.pallas_call(
        functools.partial(_expert_kernel, tm=tm),
        grid_spec=grid_spec,
        out_shape=jax.ShapeDtypeStruct((r, d), F32),
        compiler_params=_cparams("arbitrary", "arbitrary"),
        name="moe_experts",
    )(*sched, xs, wg, wu, wd)


def _combine_kernel(x_ref, y1_ref, y2_ref, gates_ref, o_ref):
    gates = gates_ref[...]
    o_ref[...] = x_ref[...] + (gates[:, 0:1] * y1_ref[...] + gates[:, 1:2] * y2_ref[...])


def _moe_combine(x, y1, y2, gates, row0, rows, tm):
    d = x.shape[1]
    tile0 = row0 // tm
    src = lambda i: (i + tile0, 0)
    wide = pl.BlockSpec((tm, d), src)
    return pl.pallas_call(
        _combine_kernel,
        grid=(rows // tm,),
        in_specs=[wide, wide, wide, pl.BlockSpec((tm, gates.shape[1]), src)],
        out_specs=pl.BlockSpec((tm, d), lambda i: (i, 0)),
        out_shape=jax.ShapeDtypeStruct((rows, d), F32),
        compiler_params=_cparams("arbitrary"),
        name="moe_combine",
    )(x, y1, y2, gates)


def _expert_schedule(idx, tm):
    t = idx.shape[0]
    i12 = idx[:, :2]
    order = jnp.argsort(i12.reshape(-1), stable=True).astype(jnp.int32)
    src = order // 2
    onehot = (i12[:, :, None] == jnp.arange(N_EXPERTS, dtype=jnp.int32)).sum(axis=1).astype(jnp.int32)
    csum = jnp.cumsum(onehot, axis=0)
    counts = csum[-1]
    ends = jnp.cumsum(counts)
    starts = ends - counts
    pos = jnp.take_along_axis(starts[None, :] + csum - onehot, i12, axis=1)

    n_tiles = (2 * t) // tm
    n_visits = n_tiles + N_EXPERTS - 1
    first_tile = starts // tm
    nvis = jnp.where(counts > 0, (ends - 1) // tm - first_tile + 1, 0)
    vend = jnp.cumsum(nvis)
    v = jnp.arange(n_visits, dtype=jnp.int32)
    active = v < vend[-1]
    ve = jnp.minimum(jnp.sum(v[:, None] >= vend[None, :], axis=1), N_EXPERTS - 1).astype(jnp.int32)
    vt = jnp.where(active, first_tile[ve] + v - (vend - nvis)[ve], n_tiles - 1).astype(jnp.int32)
    vf = jnp.concatenate([jnp.ones((1,), jnp.int32), (vt[1:] != vt[:-1]).astype(jnp.int32)])
    st = jnp.where(active, starts[ve], 0).astype(jnp.int32)
    en = jnp.where(active, ends[ve], 0).astype(jnp.int32)
    return src, pos, (vt, ve, vf, st, en)


def _ffn_moe(x, g, wg, wu, wd, wr, tm, tf, splits):
    t = x.shape[0]
    tme = _pick_tile(2 * t, (512, 256, 128, 64, 8))
    h, gates, idx = _moe_route(x, g, wr, tm)
    src, pos, sched = _expert_schedule(idx, tme)

    def rows(a, ids):
        return a.at[ids].get(mode="promise_in_bounds")

    ys = _moe_experts(rows(h, src), sched, wg, wu, wd, tme, tf)
    y1, y2 = rows(ys, pos[:, 0]), rows(ys, pos[:, 1])
    tmc = functools.reduce(math.gcd, splits, tm)
    starts = [sum(splits[:k]) for k in range(len(splits))]
    return [_moe_combine(x, y1, y2, gates, r0, n, tmc) for r0, n in zip(starts, splits)]


def _pick_tile(n, prefs):
    for p in prefs:
        if n % p == 0:
            return p
    return n


def _round_up(n, m):
    return (n + m - 1) // m * m


def kernel(x_prompt, x_sample, cache_sb_k, cache_sb_v, cache_fox_k, cache_fox_v, cache_fox_logf, cache_diff_k, cache_diff_v, norm_mix_g, norm_ffn_g, w_in, b_fox_f, fox_q_g, fox_k_g, diff_q_g, diff_k_g, diff_lam, diff_subln_g, w_out, w_ffn_gate, w_ffn_up, w_ffn_down, w_router, w_moe_gate, w_moe_up, w_moe_down):
    bp, tp, d = x_prompt.shape
    nb, ts, _ = x_sample.shape
    depth = w_in.shape[0]
    past = cache_sb_k.shape[2]
    assert bp == 1
    n_s = nb * ts
    t_all = tp + n_s
    tm_in = math.gcd(256, math.gcd(tp, n_s))
    tm_out = math.gcd(512, math.gcd(tp, n_s))
    tm = _pick_tile(t_all, (512, 256, 128, 64, 8))
    def attn_tiles(pref):
        tq_ = _pick_tile(tp, pref[:1] + (1024, 512, 256, 128))
        return tq_, _pick_tile(tq_, pref[1:] + (256, 128))

    tq_sb, tk_sb = attn_tiles(SB_TILE)
    tq_fx, tk_fx = attn_tiles(FOX_TILE)
    tq_df, tk_df = attn_tiles(DIFF_TILE)

    w3 = 3 * GROUP_W
    w_in_r = jnp.concatenate(
        [w_in[:, :, :2 * w3], w_in[:, :, 2 * w3 + N_HEADS:], w_in[:, :, 2 * w3:2 * w3 + N_HEADS],
         jnp.zeros((depth, d, HEAD_DIM - N_HEADS), w_in.dtype)], axis=-1).astype(BF16)
    b_f = jnp.pad(b_fox_f, ((0, 0), (0, HEAD_DIM - N_HEADS)))
    w_out_b = w_out.astype(BF16)
    w_fg, w_fu, w_fd = w_ffn_gate.astype(BF16), w_ffn_up.astype(BF16), w_ffn_down.astype(BF16)
    ff = w_ffn_gate.shape[-1]
    tf = _pick_tile(ff, (512, 256, 128))
    ffe = w_moe_gate.shape[-1]
    tfe = 512 if ffe >= 512 else 128
    pad_e = _round_up(ffe, tfe) - ffe
    w_mg = jnp.pad(w_moe_gate, ((0, 0), (0, 0), (0, 0), (0, pad_e))).astype(BF16)
    w_mu = jnp.pad(w_moe_up, ((0, 0), (0, 0), (0, 0), (0, pad_e))).astype(BF16)
    w_md = jnp.pad(w_moe_down, ((0, 0), (0, 0), (0, pad_e), (0, 0))).astype(BF16)

    slopes = jnp.exp2(-8.0 * jnp.arange(1, N_HEADS + 1, dtype=F32) / N_HEADS)
    slopes = jnp.broadcast_to(slopes[:, None, None], (N_HEADS, 1, HEAD_DIM))

    c_sbk, c_sbv, c_fxk, c_fxv, c_dfk, c_dfv = [
        c.reshape(depth, nb, past * N_HEADS, HEAD_DIM)
        for c in (cache_sb_k, cache_sb_v, cache_fox_k, cache_fox_v, cache_diff_k, cache_diff_v)]

    lf_past = jnp.pad(jnp.swapaxes(cache_fox_logf, 2, 3), ((0, 0), (0, 0), (0, 8 - N_HEADS), (0, 0)))
    cum_past = _cumsum_lanes(lf_past.reshape(depth * nb * 8, past)).reshape(depth, nb, 8, past)

    x_p0, x_s0 = x_prompt.reshape(tp, d), x_sample.reshape(n_s, d)
    x, y_parts = None, None
    stacked_p, stacked_s = (), ()
    logf_p, logf_s = [], []
    for l in range(depth):
        lam_init = 0.8 - 0.6 * math.exp(-0.3 * l)
        dqg = jnp.tile(diff_q_g[l], 2)[None, :]
        dkg = jnp.tile(diff_k_g[l], 2)[None, :]
        proj_args = (norm_mix_g[l][None, :], w_in_r[l], b_f[l][None, :],
                     fox_q_g[l][None, :], fox_k_g[l][None, :], dqg, dkg, tm_in)
        sg = diff_subln_g[l][None, :]
        lam = diff_lam[l]

        res = _inproj(x_p0 if l == 0 else x, 0, tp, l, depth, stacked_p, *proj_args, True)
        sbq, sbk, sbv, fxq, fxk, fxv, dfq, dfk, dfv = res[:9]
        stacked_p = tuple(res[9:9 + N_STACKED])
        logf_p.append(res[15][:, :N_HEADS])
        cum_k = res[16][:, :N_HEADS].T.reshape(N_HEADS, tp // tk_fx, 1, tk_fx)
        o_sb = _sb_prompt(sbq, sbk, sbv, tp, tq_sb, tk_sb)
        o_fx = _fox_prompt(fxq, fxk, fxv, cum_k, tp, tq_fx, tk_fx)
        o_df = _diff_prompt(dfq, dfk, dfv, slopes, lam, sg, tp, tq_df, tk_df, lam_init)
        if l == 0:
            x = _outproj_into(x_p0, None, 0, t_all, o_sb, o_fx, o_df, w_out_b[l], tm_out)
        else:
            x = _outproj(x, 0, o_sb, o_fx, o_df, w_out_b[l], tm_out)

        if l == 0:
            res = _inproj(x_s0, 0, n_s, l, depth, stacked_s, *proj_args, False)
        else:
            res = _inproj(x, tp // tm_in, n_s, l, depth, stacked_s, *proj_args, False)
        sbq, sbk, sbv, fxq, fxk, fxv, dfq, dfk, dfv = [a.reshape(nb, ts, GROUP_W) for a in res[:9]]
        stacked_s = tuple(res[9:9 + N_STACKED])
        lf_s = res[15][:, :N_HEADS]
        logf_s.append(lf_s)
        lf_new = jnp.pad(jnp.swapaxes(lf_s.reshape(nb, ts, N_HEADS), 1, 2), ((0, 0), (0, 8 - N_HEADS), (0, 0)))
        o_sb = _sb_sample(sbq, sbk, sbv, c_sbk, c_sbv, l)
        o_fx = _fox_sample(fxq, fxk, fxv, c_fxk, c_fxv, cum_past, lf_new, l)
        o_df = _diff_sample(dfq, dfk, dfv, c_dfk, c_dfv, slopes, lam, sg, lam_init, l)
        o_s = [o.reshape(n_s, GROUP_W) for o in (o_sb, o_fx, o_df)]
        if l == 0:
            x = _outproj_into(x_s0, x, tp // tm_out, t_all, *o_s, w_out_b[l], tm_out)
        else:
            x = _outproj(x, tp // tm_out, *o_s, w_out_b[l], tm_out)

        if l % 2 == 0:
            x = _ffn_dense(x, norm_ffn_g[l][None, :], w_fg[l // 2], w_fu[l // 2], w_fd[l // 2], tm, tf)
        else:
            splits = (tp, n_s) if l == depth - 1 else (t_all,)
            y_parts = _ffn_moe(x, norm_ffn_g[l][None, :], w_mg[l // 2], w_mu[l // 2], w_md[l // 2],
                               w_router[l // 2], tm, tfe, splits)
            x = y_parts[0]
    if depth % 2 == 1:
        y_parts = [x[:tp], x[tp:]]

    kv_tail = (N_HEADS, HEAD_DIM)
    sbk_p, sbv_p, fxk_p, fxv_p, dfk_p, dfv_p = [a.reshape((depth, 1, tp) + kv_tail) for a in stacked_p]
    sbk_s, sbv_s, fxk_s, fxv_s, dfk_s, dfv_s = [a.reshape((depth, nb, ts) + kv_tail) for a in stacked_s]
    lf_p = jnp.stack(logf_p).reshape(depth, 1, tp, N_HEADS)
    lf_s = jnp.stack(logf_s).reshape(depth, nb, ts, N_HEADS)
    return (y_parts[0].reshape(1, tp, d), y_parts[1].reshape(nb, ts, d),
            sbk_p, sbv_p, fxk_p, fxv_p, lf_p, dfk_p, dfv_p,
            sbk_s, sbv_s, fxk_s, fxv_s, lf_s, dfk_s, dfv_s)
```

```python
import functools
import math

import jax
import jax.numpy as jnp
from jax import lax
from jax.experimental import pallas as pl
from jax.experimental.pallas import tpu as pltpu

F32 = jnp.float32
BF16 = jnp.bfloat16

LANES = 128
HEAD_DIM = 128
N_HEADS = 4
GROUP_W = N_HEADS * HEAD_DIM
DIFF_QK = HEAD_DIM // 2
CHUNK = 64
CHUNK_SHIFT = 6
N_EXPERTS = 8
RMS_EPS = 1e-6
LOG2E = math.log2(math.e)
VMEM_LIMIT = 56 * 1024 * 1024
SB_TILE = (4096, 256)
FOX_TILE = (2048, 1024)
DIFF_TILE = (2048, 512)
NEG_INF = float("-inf")
SIGN_BIT = -2 ** 31
HIGHEST = lax.Precision.HIGHEST


def _cparams(*sem):
    return pltpu.CompilerParams(dimension_semantics=sem, vmem_limit_bytes=VMEM_LIMIT)


def _vmem_whole():
    return pl.BlockSpec(memory_space=pltpu.VMEM)


def _any_space():
    return pl.BlockSpec(memory_space=pl.ANY)


def _rms_rows(x, g):
    return x * lax.rsqrt(jnp.mean(x * x, axis=-1, keepdims=True) + RMS_EPS) * g


def _log_sigmoid(x):
    return -(jnp.maximum(-x, 0.0) + jnp.log1p(jnp.exp(-jnp.abs(x))))


def _softplus2(z):
    neg_abs = lax.bitcast_convert_type(lax.bitcast_convert_type(z, jnp.int32) | SIGN_BIT, F32)
    return jnp.maximum(z, 0.0) + jnp.log2(1.0 + jnp.exp2(neg_abs))


def _dot(a, b):
    return jnp.dot(a, b, preferred_element_type=F32)


def _dot_f32(a, b):
    return jnp.dot(a, b, preferred_element_type=F32, precision=HIGHEST)


def _dot_nt(a, b):
    return lax.dot_general(a, b, (((1,), (1,)), ((), ())), preferred_element_type=F32)


def _lane_blocks(x):
    return [x[:, c * LANES:(c + 1) * LANES] for c in range(x.shape[1] // LANES)]


def _sum_list(xs):
    acc = xs[0]
    for x in xs[1:]:
        acc = acc + x
    return acc


def _max_list(xs):
    acc = xs[0]
    for x in xs[1:]:
        acc = jnp.maximum(acc, x)
    return acc


def _head_rms(z, g, group):
    outs = []
    for hh in range(N_HEADS):
        zz = z[:, hh * HEAD_DIM:(hh + 1) * HEAD_DIM]
        sq = zz * zz
        if group == HEAD_DIM:
            r = lax.rsqrt(jnp.mean(sq, axis=-1, keepdims=True) + RMS_EPS)
        else:
            lo = lax.broadcasted_iota(jnp.int32, zz.shape, 1) < DIFF_QK
            s_lo = jnp.sum(jnp.where(lo, sq, 0.0), axis=-1, keepdims=True)
            s_hi = jnp.sum(jnp.where(lo, 0.0, sq), axis=-1, keepdims=True)
            r = jnp.where(lo, lax.rsqrt(s_lo / DIFF_QK + RMS_EPS), lax.rsqrt(s_hi / DIFF_QK + RMS_EPS))
        outs.append(zz * r * g)
    return jnp.concatenate(outs, axis=-1)


N_STACKED = 6


def _store_heads(o_ref, z):
    rows = z.shape[0]
    for hh in range(N_HEADS):
        o_ref[pl.ds(hh, rows, stride=N_HEADS), :] = z[:, hh * HEAD_DIM:(hh + 1) * HEAD_DIM]


def _inproj_kernel(*refs, with_cumsum, aliased):
    x_ref, g_ref, w_ref, bf_ref, fqg_ref, fkg_ref, dqg_ref, dkg_ref, tri_ref = refs[:9]
    outs = refs[9 + (N_STACKED if aliased else 0):]
    (sbq, sbk, sbv, fxq, fxk, fxv, dfq, dfk, dfv,
     sbk32, sbv32, fxk32, fxv32, dfk32, dfv32, logf, cumf) = outs[:17]
    h = _rms_rows(x_ref[...], g_ref[...]).astype(BF16)

    def grp(c):
        return _dot(h, w_ref[:, c * GROUP_W:(c + 1) * GROUP_W])

    z = grp(0)
    sbq[...] = (z * (HEAD_DIM ** -0.5 * LOG2E)).astype(BF16)
    z = grp(1)
    _store_heads(sbk32, z)
    sbk[...] = z.astype(BF16)
    z = grp(2)
    _store_heads(sbv32, z)
    sbv[...] = z.astype(BF16)

    z = _head_rms(grp(3), fqg_ref[...], HEAD_DIM)
    fxq[...] = (z * (HEAD_DIM ** -0.5 * LOG2E)).astype(BF16)
    z = _head_rms(grp(4), fkg_ref[...], HEAD_DIM)
    _store_heads(fxk32, z)
    fxk[...] = z.astype(BF16)
    z = grp(5)
    _store_heads(fxv32, z)
    fxv[...] = z.astype(BF16)

    z = _head_rms(grp(6), dqg_ref[...], DIFF_QK)
    dfq[...] = (z * (DIFF_QK ** -0.5 * LOG2E)).astype(BF16)
    z = _head_rms(grp(7), dkg_ref[...], DIFF_QK)
    _store_heads(dfk32, z)
    dfk[...] = z.astype(BF16)
    z = grp(8)
    _store_heads(dfv32, z)
    dfv[...] = z.astype(BF16)

    zf = _dot(h, w_ref[:, 9 * GROUP_W:9 * GROUP_W + HEAD_DIM])
    lf = _log_sigmoid(zf + bf_ref[...])
    logf[...] = lf
    if with_cumsum:
        carry_ref = outs[17]

        @pl.when(pl.program_id(0) == 0)
        def _():
            carry_ref[...] = jnp.zeros_like(carry_ref)

        cum = _dot_f32(tri_ref[...], lf) + carry_ref[0:1, :]
        cumf[...] = cum
        tm = cum.shape[0]
        carry_ref[...] = jnp.broadcast_to(cum[tm - 1:tm, :], carry_ref.shape)
    else:
        cumf[...] = lf


def _inproj(x, tile0, rows, layer, depth, stacked, g, w, bf, fqg, fkg, dqg, dkg, tm, with_cumsum):
    d = x.shape[1]
    aliased = layer > 0
    row = lambda i: (i, 0)
    full = lambda i: (0, 0)
    small = pl.BlockSpec((1, HEAD_DIM), full)
    slab = pl.BlockSpec((tm, GROUP_W), row)
    narrow = pl.BlockSpec((tm, HEAD_DIM), row)
    stacked_spec = pl.BlockSpec((None, tm * N_HEADS, HEAD_DIM), lambda i: (layer, i, 0))
    tri = jnp.tri(tm, dtype=F32)
    in_specs = [pl.BlockSpec((tm, d), lambda i: (i + tile0, 0)), pl.BlockSpec((1, d), full), _vmem_whole(),
                small, small, small, small, small, pl.BlockSpec((tm, tm), full)]
    args = [x, g, w, bf, fqg, fkg, dqg, dkg, tri]
    aliases = {}
    if aliased:
        in_specs += [_any_space()] * N_STACKED
        aliases = {len(args) + k: 9 + k for k in range(N_STACKED)}
        args += list(stacked)
    out_bf = jax.ShapeDtypeStruct((rows, GROUP_W), BF16)
    out_n = jax.ShapeDtypeStruct((rows, HEAD_DIM), F32)
    out_st = jax.ShapeDtypeStruct((depth, rows * N_HEADS, HEAD_DIM), F32)
    return pl.pallas_call(
        functools.partial(_inproj_kernel, with_cumsum=with_cumsum, aliased=aliased),
        grid=(rows // tm,),
        in_specs=in_specs,
        out_specs=[slab] * 9 + [stacked_spec] * N_STACKED + [narrow, narrow],
        out_shape=[out_bf] * 9 + [out_st] * N_STACKED + [out_n, out_n],
        scratch_shapes=[pltpu.VMEM((8, HEAD_DIM), F32)] if with_cumsum else [],
        input_output_aliases=aliases,
        compiler_params=_cparams("arbitrary"),
        name="inproj",
    )(*args)


def _sb_prompt_kernel(q_ref, k_ref, v_ref, tri_ref, o_ref, c_ref, acc_ref, *, tq, tk):
    i = pl.program_id(1)
    ratio = tq // tk
    row = lax.broadcasted_iota(jnp.int32, (tk, tk), 0)
    col = lax.broadcasted_iota(jnp.int32, (tk, tk), 1)
    strictly_before = col < row
    c_ref[...] = jnp.zeros_like(c_ref)
    acc_ref[...] = jnp.zeros_like(acc_ref)

    def step(j, r0, r1, masked):
        kc, vc = _kv_chunk(k_ref, v_ref, j, tk)
        z = _dot_nt(q_ref[r0:r1, :], kc)
        cost = _softplus2(z)
        if masked:
            cost = jnp.where(strictly_before, cost, 0.0)
        incl = _dot(cost.astype(BF16), tri_ref[...])
        w = jnp.exp2(z - incl)
        if masked:
            w = jnp.where(strictly_before, w, 0.0)
        carried = c_ref[r0:r1, :]
        acc_ref[r0:r1, :] += jnp.exp2(-carried) * _dot(w.astype(BF16), vc)
        c_ref[r0:r1, :] = carried + incl[:, 0:1]

    for d in reversed(range(ratio)):
        step(i * ratio + d, d * tk, (d + 1) * tk, True)
        if d + 1 < ratio:
            step(i * ratio + d, (d + 1) * tk, tq, False)

    def body(jj, carry):
        step(i * ratio - 1 - jj, 0, tq, False)
        return carry

    lax.fori_loop(0, i * ratio, body, 0)
    o_ref[...] = acc_ref[...].astype(BF16)


def _sb_prompt(q, k, v, tp, tq, tk):
    tri = jnp.tri(tk, dtype=BF16)
    return pl.pallas_call(
        functools.partial(_sb_prompt_kernel, tq=tq, tk=tk),
        grid=(N_HEADS, tp // tq),
        in_specs=[pl.BlockSpec((tq, HEAD_DIM), lambda h, i: (i, h)),
                  _kv_spec(tp), _kv_spec(tp),
                  pl.BlockSpec((tk, tk), lambda h, i: (0, 0))],
        out_specs=pl.BlockSpec((tq, HEAD_DIM), lambda h, i: (i, h)),
        out_shape=jax.ShapeDtypeStruct((tp, GROUP_W), BF16),
        scratch_shapes=[pltpu.VMEM((tq, LANES), F32), pltpu.VMEM((tq, HEAD_DIM), F32)],
        compiler_params=_cparams("arbitrary", "arbitrary"),
        name="sb_prompt",
    )(q, k, v, tri)


def _softmax_step(s, vc, m_ref, l_ref, acc_ref, r0, r1):
    blocks = _lane_blocks(s)
    m_prev = m_ref[r0:r1, :]
    m_new = jnp.maximum(m_prev, jnp.max(_max_list(blocks), axis=-1, keepdims=True))
    alpha = jnp.exp2(m_prev - m_new)
    ps = [jnp.exp2(b - m_new) for b in blocks]
    l_ref[r0:r1, :] = alpha * l_ref[r0:r1, :] + _sum_list(ps)
    acc_ref[r0:r1, :] = alpha * acc_ref[r0:r1, :] + _dot(jnp.concatenate(ps, axis=1).astype(BF16), vc)
    m_ref[r0:r1, :] = m_new


def _softmax_init(m_ref, l_ref, acc_ref):
    m_ref[...] = jnp.full_like(m_ref, NEG_INF)
    l_ref[...] = jnp.zeros_like(l_ref)
    acc_ref[...] = jnp.zeros_like(acc_ref)


def _softmax_result(l_ref, acc_ref):
    return acc_ref[...] * (1.0 / jnp.sum(l_ref[...], axis=-1, keepdims=True))


def _diagonal_steps(step, n_full, ratio, tq, tk):
    for d in range(ratio):
        step(n_full + d, d * tk, (d + 1) * tk, True)
        if d + 1 < ratio:
            step(n_full + d, (d + 1) * tk, tq, False)


def _kv_chunk(k_ref, v_ref, j, tk):
    start = pl.multiple_of(j * tk, tk)
    return k_ref[pl.ds(start, tk), :], v_ref[pl.ds(start, tk), :]


def _fox_prompt_kernel(q_ref, k_ref, v_ref, fk_ref, o_ref, m_ref, l_ref, acc_ref, *, tq, tk):
    i = pl.program_id(1)
    ratio = tq // tk
    n_full = i * ratio
    f0 = fk_ref[n_full][:, 0:1]
    _softmax_init(m_ref, l_ref, acc_ref)
    row = lax.broadcasted_iota(jnp.int32, (tk, tk), 0)
    col = lax.broadcasted_iota(jnp.int32, (tk, tk), 1)

    def step(j, r0, r1, masked):
        kc, vc = _kv_chunk(k_ref, v_ref, j, tk)
        s = _dot_nt(q_ref[r0:r1, :], kc) + (f0 - fk_ref[j]) * LOG2E
        if masked:
            s = jnp.where(col <= row, s, NEG_INF)
        _softmax_step(s, vc, m_ref, l_ref, acc_ref, r0, r1)

    _diagonal_steps(step, n_full, ratio, tq, tk)

    def body(j, carry):
        step(j, 0, tq, False)
        return carry

    lax.fori_loop(0, n_full, body, 0)
    o_ref[...] = _softmax_result(l_ref, acc_ref).astype(BF16)


def _kv_spec(tp):
    return pl.BlockSpec((tp, HEAD_DIM), lambda h, i: (0, h), pipeline_mode=pl.Buffered(1))


def _fox_prompt(q, k, v, fk, tp, tq, tk):
    stat = lambda: pltpu.VMEM((tq, LANES), F32)
    return pl.pallas_call(
        functools.partial(_fox_prompt_kernel, tq=tq, tk=tk),
        grid=(N_HEADS, tp // tq),
        in_specs=[pl.BlockSpec((tq, HEAD_DIM), lambda h, i: (i, h)),
                  _kv_spec(tp), _kv_spec(tp),
                  pl.BlockSpec((None, tp // tk, 1, tk), lambda h, i: (h, 0, 0, 0))],
        out_specs=pl.BlockSpec((tq, HEAD_DIM), lambda h, i: (i, h)),
        out_shape=jax.ShapeDtypeStruct((tp, GROUP_W), BF16),
        scratch_shapes=[stat(), stat(), pltpu.VMEM((tq, HEAD_DIM), F32)],
        compiler_params=_cparams("arbitrary", "arbitrary"),
        name="fox_prompt",
    )(q, k, v, fk)


def _diff_lambda(lam_ref, lam_init):
    lv = lam_ref[...]
    a = jnp.sum(lv[0:1, :] * lv[1:2, :], axis=-1, keepdims=True)
    b = jnp.sum(lv[2:3, :] * lv[3:4, :], axis=-1, keepdims=True)
    return jnp.exp(a) - jnp.exp(b) + lam_init


def _split_halves(q):
    first = lax.broadcasted_iota(jnp.int32, q.shape, 1) < DIFF_QK
    zero = jnp.zeros_like(q)
    return jnp.where(first, q, zero), jnp.where(first, zero, q)


def _diff_prompt_kernel(q_ref, k_ref, v_ref, slope_ref, lam_ref, sg_ref, o_ref,
                        m1_ref, l1_ref, a1_ref, m2_ref, l2_ref, a2_ref, *, tq, tk, lam_init):
    i = pl.program_id(1)
    ratio = tq // tk
    n_full = i * ratio
    slope = slope_ref[:, 0:1] * LOG2E
    _softmax_init(m1_ref, l1_ref, a1_ref)
    _softmax_init(m2_ref, l2_ref, a2_ref)
    row = lax.broadcasted_iota(jnp.int32, (tk, tk), 0)
    col = lax.broadcasted_iota(jnp.int32, (tk, tk), 1)
    visible = lax.shift_right_logical(col, CHUNK_SHIFT) <= lax.shift_right_logical(row, CHUNK_SHIFT)
    own_rel = row - jnp.abs(row - col)
    col_row = lax.broadcasted_iota(jnp.int32, (1, tk), 1)
    q1_all, q2_all = _split_halves(q_ref[...])

    def step(j, r0, r1, masked):
        kc, vc = _kv_chunk(k_ref, v_ref, j, tk)
        q1, q2 = q1_all[r0:r1, :], q2_all[r0:r1, :]
        first_key = j * tk - i * tq
        if masked:
            bias = jnp.where(visible, slope * (own_rel + first_key).astype(F32), NEG_INF)
        else:
            bias = slope * (col_row + first_key).astype(F32)
        _softmax_step(_dot_nt(q1, kc) + bias, vc, m1_ref, l1_ref, a1_ref, r0, r1)
        _softmax_step(_dot_nt(q2, kc) + bias, vc, m2_ref, l2_ref, a2_ref, r0, r1)

    _diagonal_steps(step, n_full, ratio, tq, tk)

    def body(j, carry):
        step(j, 0, tq, False)
        return carry

    lax.fori_loop(0, n_full, body, 0)
    lam = _diff_lambda(lam_ref, lam_init)
    o = _softmax_result(l1_ref, a1_ref) - lam * _softmax_result(l2_ref, a2_ref)
    o_ref[...] = (_rms_rows(o, sg_ref[...]) * (1.0 - lam_init)).astype(BF16)


def _diff_prompt(q, k, v, slopes, lam, subln_g, tp, tq, tk, lam_init):
    stat = lambda: pltpu.VMEM((tq, LANES), F32)
    acc = lambda: pltpu.VMEM((tq, HEAD_DIM), F32)
    return pl.pallas_call(
        functools.partial(_diff_prompt_kernel, tq=tq, tk=tk, lam_init=lam_init),
        grid=(N_HEADS, tp // tq),
        in_specs=[pl.BlockSpec((tq, HEAD_DIM), lambda h, i: (i, h)),
                  _kv_spec(tp), _kv_spec(tp),
                  pl.BlockSpec((None, 1, HEAD_DIM), lambda h, i: (h, 0, 0)),
                  pl.BlockSpec((4, DIFF_QK), lambda h, i: (0, 0)),
                  pl.BlockSpec((1, HEAD_DIM), lambda h, i: (0, 0))],
        out_specs=pl.BlockSpec((tq, HEAD_DIM), lambda h, i: (i, h)),
        out_shape=jax.ShapeDtypeStruct((tp, GROUP_W), BF16),
        scratch_shapes=[stat(), stat(), acc(), stat(), stat(), acc()],
        compiler_params=_cparams("arbitrary", "arbitrary"),
        name="diff_prompt",
    )(q, k, v, slopes, lam, subln_g)


def _split_bf16(x):
    hi = x.astype(BF16)
    return hi, (x - hi.astype(F32)).astype(BF16)


def _sb_sample_kernel(q_ref, kn_ref, vn_ref, ck_ref, cv_ref, trip_ref, trin_ref, o_ref, *, past, t):
    rown = lax.broadcasted_iota(jnp.int32, (t, t), 0)
    coln = lax.broadcasted_iota(jnp.int32, (t, t), 1)
    allowed_n = coln < rown
    heads = range(N_HEADS)
    sls = [slice(hh * HEAD_DIM, (hh + 1) * HEAD_DIM) for hh in heads]
    zps = [_dot_nt(q_ref[:, sls[hh]], _cache_head(ck_ref, hh)) for hh in heads]
    zns = [_dot_nt(q_ref[:, sls[hh]], kn_ref[:, sls[hh]]) for hh in heads]
    cps = [_softplus2(zp) for zp in zps]
    cns = [jnp.where(allowed_n, _softplus2(zn), 0.0) for zn in zns]

    def suffix_sums(costs, tri_ref):
        hi, lo = _split_bf16(jnp.concatenate(costs, axis=0))
        both = _dot(jnp.concatenate([hi, lo], axis=0), tri_ref[...])
        return both[:N_HEADS * t, :] + both[N_HEADS * t:, :]

    later_p = suffix_sums(cps, trip_ref)
    later_n = suffix_sums(cns, trin_ref)
    outs = []
    for hh in heads:
        rows = slice(hh * t, (hh + 1) * t)
        after_p = later_p[rows, :] + jnp.sum(cns[hh], axis=-1, keepdims=True)
        wp = jnp.exp2(zps[hh] - cps[hh] - after_p)
        wn = jnp.where(allowed_n, jnp.exp2(zns[hh] - cns[hh] - later_n[rows, :]), 0.0)
        outs.append(_dot(wp.astype(BF16), _cache_head(cv_ref, hh)) + _dot(wn.astype(BF16), vn_ref[:, sls[hh]]))
    o_ref[...] = jnp.concatenate(outs, axis=-1).astype(BF16)


def _cache_head(c_ref, hh):
    past = c_ref.shape[0] // N_HEADS
    return c_ref[pl.ds(hh, past, stride=N_HEADS), :].astype(BF16)


def _sample_specs(t, past, layer):
    new = pl.BlockSpec((None, t, GROUP_W), lambda b: (b, 0, 0))
    cache = pl.BlockSpec((None, None, past * N_HEADS, HEAD_DIM), lambda b: (layer, b, 0, 0))
    return new, cache


def _sb_sample(q, kn, vn, ck, cv, layer):
    nb, t, _ = q.shape
    past = ck.shape[2] // N_HEADS
    new, cache = _sample_specs(t, past, layer)
    trip = jnp.tri(past, k=-1, dtype=BF16)
    trin = jnp.tri(t, k=-1, dtype=BF16)
    return pl.pallas_call(
        functools.partial(_sb_sample_kernel, past=past, t=t),
        grid=(nb,),
        in_specs=[new, new, new, cache, cache,
                  pl.BlockSpec((past, past), lambda b: (0, 0)),
                  pl.BlockSpec((t, t), lambda b: (0, 0))],
        out_specs=new,
        out_shape=jax.ShapeDtypeStruct((nb, t, GROUP_W), BF16),
        compiler_params=_cparams("arbitrary"),
        name="sb_sample",
    )(q, kn, vn, ck, cv, trip, trin)


def _cumsum_kernel(x_ref, tri_ref, o_ref):
    o_ref[...] = _dot_f32(x_ref[...], tri_ref[...])


def _cumsum_lanes(x):
    rows, n = x.shape
    tri = jnp.tri(n, dtype=F32).T
    return pl.pallas_call(
        _cumsum_kernel,
        in_specs=[_vmem_whole(), _vmem_whole()],
        out_specs=_vmem_whole(),
        out_shape=jax.ShapeDtypeStruct((rows, n), F32),
        compiler_params=pltpu.CompilerParams(vmem_limit_bytes=VMEM_LIMIT),
        name="cache_logf_cumsum",
    )(x, tri)


def _fox_sample_kernel(q_ref, kn_ref, vn_ref, ck_ref, cv_ref, cump_ref, lfn_ref, trin_ref, o_ref, *, past, t):
    rown = lax.broadcasted_iota(jnp.int32, (t, t), 0)
    coln = lax.broadcasted_iota(jnp.int32, (t, t), 1)
    cum_p = cump_ref[...]
    total = cum_p[:, past - 1:past]
    bias_p = (total - cum_p) * LOG2E
    bias_n = -_dot_f32(lfn_ref[...], trin_ref[...]) * LOG2E
    outs = []
    for hh in range(N_HEADS):
        sl = slice(hh * HEAD_DIM, (hh + 1) * HEAD_DIM)
        q = q_ref[:, sl]
        sp = _dot_nt(q, _cache_head(ck_ref, hh)) + bias_p[hh:hh + 1, :]
        sn = _dot_nt(q, kn_ref[:, sl]) + bias_n[hh:hh + 1, :]
        sn = jnp.where(coln <= rown, sn, NEG_INF)
        m = jnp.maximum(jnp.max(sp, axis=-1, keepdims=True), jnp.max(sn, axis=-1, keepdims=True))
        pp = jnp.exp2(sp - m)
        pn = jnp.exp2(sn - m)
        inv = 1.0 / (jnp.sum(pp, axis=-1, keepdims=True) + jnp.sum(pn, axis=-1, keepdims=True))
        outs.append(_dot((pp * inv).astype(BF16), _cache_head(cv_ref, hh))
                    + _dot((pn * inv).astype(BF16), vn_ref[:, sl]))
    o_ref[...] = jnp.concatenate(outs, axis=-1).astype(BF16)


def _fox_sample(q, kn, vn, ck, cv, cum_past, lf_new, layer):
    nb, t, _ = q.shape
    past = ck.shape[2] // N_HEADS
    new, cache = _sample_specs(t, past, layer)
    trin = jnp.tri(t, dtype=F32).T
    return pl.pallas_call(
        functools.partial(_fox_sample_kernel, past=past, t=t),
        grid=(nb,),
        in_specs=[new, new, new, cache, cache,
                  pl.BlockSpec((None, None, 8, past), lambda b: (layer, b, 0, 0)),
                  pl.BlockSpec((None, 8, t), lambda b: (b, 0, 0)),
                  pl.BlockSpec((t, t), lambda b: (0, 0))],
        out_specs=new,
        out_shape=jax.ShapeDtypeStruct((nb, t, GROUP_W), BF16),
        compiler_params=_cparams("arbitrary"),
        name="fox_sample",
    )(q, kn, vn, ck, cv, cum_past, lf_new, trin)


def _diff_sample_kernel(q_ref, kn_ref, vn_ref, ck_ref, cv_ref, slope_ref, lam_ref, sg_ref, o_ref,
                        *, past, t, lam_init):
    lam = _diff_lambda(lam_ref, lam_init)
    q_pos_p = lax.broadcasted_iota(jnp.int32, (t, past), 0) + past
    k_pos_p = lax.broadcasted_iota(jnp.int32, (t, past), 1)
    q_pos_n = lax.broadcasted_iota(jnp.int32, (t, t), 0) + past
    k_pos_n = lax.broadcasted_iota(jnp.int32, (t, t), 1) + past
    vis_p = lax.shift_right_logical(k_pos_p, CHUNK_SHIFT) <= lax.shift_right_logical(q_pos_p, CHUNK_SHIFT)
    vis_n = lax.shift_right_logical(k_pos_n, CHUNK_SHIFT) <= lax.shift_right_logical(q_pos_n, CHUNK_SHIFT)
    dist_p = jnp.abs(q_pos_p - k_pos_p).astype(F32)
    dist_n = jnp.abs(q_pos_n - k_pos_n).astype(F32)
    outs = []
    for hh in range(N_HEADS):
        sl = slice(hh * HEAD_DIM, (hh + 1) * HEAD_DIM)
        slope = slope_ref[hh][:, 0:1] * LOG2E
        bias_p = jnp.where(vis_p, -slope * dist_p, NEG_INF)
        bias_n = jnp.where(vis_n, -slope * dist_n, NEG_INF)
        kp = _cache_head(ck_ref, hh)
        kn = kn_ref[:, sl]
        p_p, p_n = None, None
        for qh, coef in zip(_split_halves(q_ref[:, sl]), (None, lam)):
            sp = _dot_nt(qh, kp) + bias_p
            sn = _dot_nt(qh, kn) + bias_n
            m = jnp.maximum(jnp.max(sp, axis=-1, keepdims=True), jnp.max(sn, axis=-1, keepdims=True))
            ep = jnp.exp2(sp - m)
            en = jnp.exp2(sn - m)
            inv = 1.0 / (jnp.sum(ep, axis=-1, keepdims=True) + jnp.sum(en, axis=-1, keepdims=True))
            if coef is None:
                p_p, p_n = ep * inv, en * inv
            else:
                p_p, p_n = p_p - coef * (ep * inv), p_n - coef * (en * inv)
        o = _dot(p_p.astype(BF16), _cache_head(cv_ref, hh)) + _dot(p_n.astype(BF16), vn_ref[:, sl])
        outs.append(_rms_rows(o, sg_ref[...]) * (1.0 - lam_init))
    o_ref[...] = jnp.concatenate(outs, axis=-1).astype(BF16)


def _diff_sample(q, kn, vn, ck, cv, slopes, lam, subln_g, lam_init, layer):
    nb, t, _ = q.shape
    past = ck.shape[2] // N_HEADS
    new, cache = _sample_specs(t, past, layer)
    return pl.pallas_call(
        functools.partial(_diff_sample_kernel, past=past, t=t, lam_init=lam_init),
        grid=(nb,),
        in_specs=[new, new, new, cache, cache,
                  pl.BlockSpec((N_HEADS, 1, HEAD_DIM), lambda b: (0, 0, 0)),
                  pl.BlockSpec((4, DIFF_QK), lambda b: (0, 0)),
                  pl.BlockSpec((1, HEAD_DIM), lambda b: (0, 0))],
        out_specs=new,
        out_shape=jax.ShapeDtypeStruct((nb, t, GROUP_W), BF16),
        compiler_params=_cparams("arbitrary"),
        name="diff_sample",
    )(q, kn, vn, ck, cv, slopes, lam, subln_g)


def _outproj_kernel(x_ref, osb_ref, ofx_ref, odf_ref, w_ref, *rest):
    o_ref = rest[-1]
    y = _dot(osb_ref[...], w_ref[0:GROUP_W, :])
    y += _dot(ofx_ref[...], w_ref[GROUP_W:2 * GROUP_W, :])
    y += _dot(odf_ref[...], w_ref[2 * GROUP_W:3 * GROUP_W, :])
    o_ref[...] = x_ref[...] + y


def _outproj(x, tile0, o_sb, o_fox, o_diff, w, tm):
    rows = o_sb.shape[0]
    d = x.shape[1]
    row = lambda i: (i, 0)
    xrow = lambda i: (i + tile0, 0)
    slab = pl.BlockSpec((tm, GROUP_W), row)
    return pl.pallas_call(
        _outproj_kernel,
        grid=(rows // tm,),
        in_specs=[pl.BlockSpec((tm, d), xrow), slab, slab, slab, _vmem_whole()],
        out_specs=pl.BlockSpec((tm, d), xrow),
        out_shape=jax.ShapeDtypeStruct(x.shape, F32),
        input_output_aliases={0: 0},
        compiler_params=_cparams("arbitrary"),
        name="outproj",
    )(x, o_sb, o_fox, o_diff, w)


def _outproj_into(x_rows, stream, tile0, total_rows, o_sb, o_fox, o_diff, w, tm):
    rows, d = x_rows.shape
    row = lambda i: (i, 0)
    slab = pl.BlockSpec((tm, GROUP_W), row)
    in_specs = [pl.BlockSpec((tm, d), row), slab, slab, slab, _vmem_whole()]
    args = [x_rows, o_sb, o_fox, o_diff, w]
    aliases = {}
    if stream is not None:
        in_specs.append(_any_space())
        aliases = {len(args): 0}
        args.append(stream)
    return pl.pallas_call(
        _outproj_kernel,
        grid=(rows // tm,),
        in_specs=in_specs,
        out_specs=pl.BlockSpec((tm, d), lambda i: (i + tile0, 0)),
        out_shape=jax.ShapeDtypeStruct((total_rows, d), F32),
        input_output_aliases=aliases,
        compiler_params=_cparams("arbitrary"),
        name="outproj",
    )(*args)


def _swiglu_part(h, wg_ref, wu_ref, wd_ref):
    gate = _dot(h, wg_ref[...])
    up = _dot(h, wu_ref[...])
    a = gate * jax.nn.sigmoid(gate) * up
    return _dot(a.astype(BF16), wd_ref[...])


def _ffn_kernel(x_ref, g_ref, wg_ref, wu_ref, wd_ref, o_ref, h_ref):
    f = pl.program_id(1)

    @pl.when(f == 0)
    def _():
        h_ref[...] = _rms_rows(x_ref[...], g_ref[...]).astype(BF16)

    part = _swiglu_part(h_ref[...], wg_ref, wu_ref, wd_ref)

    @pl.when(f == 0)
    def _():
        o_ref[...] = x_ref[...] + part

    @pl.when(f != 0)
    def _():
        o_ref[...] += part


def _ffn_dense(x, g, wg, wu, wd, tm, tf):
    t, d = x.shape
    nf = wg.shape[1] // tf
    return pl.pallas_call(
        _ffn_kernel,
        grid=(t // tm, nf),
        in_specs=[pl.BlockSpec((tm, d), lambda i, f: (i, 0)),
                  pl.BlockSpec((1, d), lambda i, f: (0, 0)),
                  pl.BlockSpec((d, tf), lambda i, f: (0, f)),
                  pl.BlockSpec((d, tf), lambda i, f: (0, f)),
                  pl.BlockSpec((tf, d), lambda i, f: (f, 0))],
        out_specs=pl.BlockSpec((tm, d), lambda i, f: (i, 0)),
        out_shape=jax.ShapeDtypeStruct((t, d), F32),
        scratch_shapes=[pltpu.VMEM((tm, d), BF16)],
        compiler_params=_cparams("arbitrary", "arbitrary"),
        name="ffn_dense",
    )(x, g, wg, wu, wd)


def _route_kernel(x_ref, g_ref, wr_ref, h_ref, gates_ref, idx_ref):
    hf = _rms_rows(x_ref[...], g_ref[...])
    h_ref[...] = hf.astype(BF16)
    logits = _dot_f32(hf, wr_ref[...])
    lane = lax.broadcasted_iota(jnp.int32, logits.shape, 1)
    m1 = jnp.max(logits, axis=-1, keepdims=True)
    i1 = jnp.min(jnp.where(logits == m1, lane, N_EXPERTS), axis=-1, keepdims=True)
    rest = jnp.where(lane == i1, NEG_INF, logits)
    m2 = jnp.max(rest, axis=-1, keepdims=True)
    i2 = jnp.min(jnp.where(rest == m2, lane, N_EXPERTS), axis=-1, keepdims=True)
    e2 = jnp.exp(m2 - m1)
    inv = 1.0 / (1.0 + e2)
    gates_ref[...] = jnp.where(lane == 0, inv, jnp.where(lane == 1, e2 * inv, 0.0))
    idx_ref[...] = jnp.where(lane == 0, i1, jnp.where(lane == 1, i2, 0))


def _moe_route(x, g, wr, tm):
    t, d = x.shape
    ne = wr.shape[1]
    row = lambda i: (i, 0)
    return pl.pallas_call(
        _route_kernel,
        grid=(t // tm,),
        in_specs=[pl.BlockSpec((tm, d), row), pl.BlockSpec((1, d), lambda i: (0, 0)),
                  pl.BlockSpec((d, ne), lambda i: (0, 0))],
        out_specs=[pl.BlockSpec((tm, d), row), pl.BlockSpec((tm, ne), row), pl.BlockSpec((tm, ne), row)],
        out_shape=[jax.ShapeDtypeStruct((t, d), BF16), jax.ShapeDtypeStruct((t, ne), F32),
                   jax.ShapeDtypeStruct((t, ne), jnp.int32)],
        compiler_params=_cparams("arbitrary"),
        name="moe_route",
    )(x, g, wr)


def _expert_kernel(vt_ref, ve_ref, vf_ref, st_ref, en_ref, xs_ref, wg_ref, wu_ref, wd_ref, o_ref, h_ref, *, tm):
    v = pl.program_id(0)
    f = pl.program_id(1)
    lo = jnp.maximum(st_ref[v], vt_ref[v] * tm)
    hi = jnp.minimum(en_ref[v], (vt_ref[v] + 1) * tm)

    @pl.when(hi > lo)
    def _():
        @pl.when(f == 0)
        def _():
            rows = lax.broadcasted_iota(jnp.int32, (tm, 1), 0) + vt_ref[v] * tm
            mine = (rows >= lo) & (rows < hi)
            h_ref[...] = jnp.where(mine, xs_ref[...], jnp.zeros_like(xs_ref))

        part = _swiglu_part(h_ref[...], wg_ref, wu_ref, wd_ref)
        fresh = (vf_ref[v] == 1) & (f == 0)

        @pl.when(fresh)
        def _():
            o_ref[...] = part

        @pl.when(jnp.logical_not(fresh))
        def _():
            o_ref[...] += part


def _moe_experts(xs, sched, wg, wu, wd, tm, tf):
    r, d = xs.shape
    ne, _, fe = wg.shape
    tpe = fe // tf
    n_visits = sched[0].shape[0]
    grid_spec = pltpu.PrefetchScalarGridSpec(
        num_scalar_prefetch=5,
        grid=(n_visits, tpe),
        in_specs=[pl.BlockSpec((tm, d), lambda v, f, vt, ve, vf, st, en: (vt[v], 0)),
                  pl.BlockSpec((None, d, tf), lambda v, f, vt, ve, vf, st, en: (ve[v], 0, f)),
                  pl.BlockSpec((None, d, tf), lambda v, f, vt, ve, vf, st, en: (ve[v], 0, f)),
                  pl.BlockSpec((None, tf, d), lambda v, f, vt, ve, vf, st, en: (ve[v], f, 0))],
        out_specs=pl.BlockSpec((tm, d), lambda v, f, vt, ve, vf, st, en: (vt[v], 0)),
        scratch_shapes=[pltpu.VMEM((tm, d), BF16)])
    return pl.pallas_call(
        functools.partial(_expert_kernel, tm=tm),
        grid_spec=grid_spec,
        out_shape=jax.ShapeDtypeStruct((r, d), F32),
        compiler_params=_cparams("arbitrary", "arbitrary"),
        name="moe_experts",
    )(*sched, xs, wg, wu, wd)


def _combine_kernel(x_ref, y1_ref, y2_ref, gates_ref, o_ref):
    gates = gates_ref[...]
    o_ref[...] = x_ref[...] + (gates[:, 0:1] * y1_ref[...] + gates[:, 1:2] * y2_ref[...])


def _moe_combine(x, y1, y2, gates, row0, rows, tm):
    d = x.shape[1]
    tile0 = row0 // tm
    src = lambda i: (i + tile0, 0)
    wide = pl.BlockSpec((tm, d), src)
    return pl.pallas_call(
        _combine_kernel,
        grid=(rows // tm,),
        in_specs=[wide, wide, wide, pl.BlockSpec((tm, gates.shape[1]), src)],
        out_specs=pl.BlockSpec((tm, d), lambda i: (i, 0)),
        out_shape=jax.ShapeDtypeStruct((rows, d), F32),
        compiler_params=_cparams("arbitrary"),
        name="moe_combine",
    )(x, y1, y2, gates)


def _expert_schedule(idx, tm):
    t = idx.shape[0]
    i12 = idx[:, :2]
    order = jnp.argsort(i12.reshape(-1), stable=True).astype(jnp.int32)
    src = order // 2
    onehot = (i12[:, :, None] == jnp.arange(N_EXPERTS, dtype=jnp.int32)).sum(axis=1).astype(jnp.int32)
    csum = jnp.cumsum(onehot, axis=0)
    counts = csum[-1]
    ends = jnp.cumsum(counts)
    starts = ends - counts
    pos = jnp.take_along_axis(starts[None, :] + csum - onehot, i12, axis=1)

    n_tiles = (2 * t) // tm
    n_visits = n_tiles + N_EXPERTS - 1
    first_tile = starts // tm
    nvis = jnp.where(counts > 0, (ends - 1) // tm - first_tile + 1, 0)
    vend = jnp.cumsum(nvis)
    v = jnp.arange(n_visits, dtype=jnp.int32)
    active = v < vend[-1]
    ve = jnp.minimum(jnp.sum(v[:, None] >= vend[None, :], axis=1), N_EXPERTS - 1).astype(jnp.int32)
    vt = jnp.where(active, first_tile[ve] + v - (vend - nvis)[ve], n_tiles - 1).astype(jnp.int32)
    vf = jnp.concatenate([jnp.ones((1,), jnp.int32), (vt[1:] != vt[:-1]).astype(jnp.int32)])
    st = jnp.where(active, starts[ve], 0).astype(jnp.int32)
    en = jnp.where(active, ends[ve], 0).astype(jnp.int32)
    return src, pos, (vt, ve, vf, st, en)


def _ffn_moe(x, g, wg, wu, wd, wr, tm, tf, splits):
    t = x.shape[0]
    tme = _pick_tile(2 * t, (512, 256, 128, 64, 8))
    h, gates, idx = _moe_route(x, g, wr, tm)
    src, pos, sched = _expert_schedule(idx, tme)

    def rows(a, ids):
        return a.at[ids].get(mode="promise_in_bounds")

    ys = _moe_experts(rows(h, src), sched, wg, wu, wd, tme, tf)
    y1, y2 = rows(ys, pos[:, 0]), rows(ys, pos[:, 1])
    tmc = functools.reduce(math.gcd, splits, tm)
    starts = [sum(splits[:k]) for k in range(len(splits))]
    return [_moe_combine(x, y1, y2, gates, r0, n, tmc) for r0, n in zip(starts, splits)]


def _pick_tile(n, prefs):
    for p in prefs:
        if n % p == 0:
            return p
    return n


def _round_up(n, m):
    return (n + m - 1) // m * m


def kernel(x_prompt, x_sample, cache_sb_k, cache_sb_v, cache_fox_k, cache_fox_v, cache_fox_logf, cache_diff_k, cache_diff_v, norm_mix_g, norm_ffn_g, w_in, b_fox_f, fox_q_g, fox_k_g, diff_q_g, diff_k_g, diff_lam, diff_subln_g, w_out, w_ffn_gate, w_ffn_up, w_ffn_down, w_router, w_moe_gate, w_moe_up, w_moe_down):
    bp, tp, d = x_prompt.shape
    nb, ts, _ = x_sample.shape
    depth = w_in.shape[0]
    past = cache_sb_k.shape[2]
    assert bp == 1
    n_s = nb * ts
    t_all = tp + n_s
    tm_in = math.gcd(256, math.gcd(tp, n_s))
    tm_out = math.gcd(512, math.gcd(tp, n_s))
    tm = _pick_tile(t_all, (512, 256, 128, 64, 8))
    def attn_tiles(pref):
        tq_ = _pick_tile(tp, pref[:1] + (1024, 512, 256, 128))
        return tq_, _pick_tile(tq_, pref[1:] + (256, 128))

    tq_sb, tk_sb = attn_tiles(SB_TILE)
    tq_fx, tk_fx = attn_tiles(FOX_TILE)
    tq_df, tk_df = attn_tiles(DIFF_TILE)

    w3 = 3 * GROUP_W
    w_in_r = jnp.concatenate(
        [w_in[:, :, :2 * w3], w_in[:, :, 2 * w3 + N_HEADS:], w_in[:, :, 2 * w3:2 * w3 + N_HEADS],
         jnp.zeros((depth, d, HEAD_DIM - N_HEADS), w_in.dtype)], axis=-1).astype(BF16)
    b_f = jnp.pad(b_fox_f, ((0, 0), (0, HEAD_DIM - N_HEADS)))
    w_out_b = w_out.astype(BF16)
    w_fg, w_fu, w_fd = w_ffn_gate.astype(BF16), w_ffn_up.astype(BF16), w_ffn_down.astype(BF16)
    ff = w_ffn_gate.shape[-1]
    tf = _pick_tile(ff, (512, 256, 128))
    ffe = w_moe_gate.shape[-1]
    tfe = 512 if ffe >= 512 else 128
    pad_e = _round_up(ffe, tfe) - ffe
    w_mg = jnp.pad(w_moe_gate, ((0, 0), (0, 0), (0, 0), (0, pad_e))).astype(BF16)
    w_mu = jnp.pad(w_moe_up, ((0, 0), (0, 0), (0, 0), (0, pad_e))).astype(BF16)
    w_md = jnp.pad(w_moe_down, ((0, 0), (0, 0), (0, pad_e), (0, 0))).astype(BF16)

    slopes = jnp.exp2(-8.0 * jnp.arange(1, N_HEADS + 1, dtype=F32) / N_HEADS)
    slopes = jnp.broadcast_to(slopes[:, None, None], (N_HEADS, 1, HEAD_DIM))

    c_sbk, c_sbv, c_fxk, c_fxv, c_dfk, c_dfv = [
        c.reshape(depth, nb, past * N_HEADS, HEAD_DIM)
        for c in (cache_sb_k, cache_sb_v, cache_fox_k, cache_fox_v, cache_diff_k, cache_diff_v)]

    lf_past = jnp.pad(jnp.swapaxes(cache_fox_logf, 2, 3), ((0, 0), (0, 0), (0, 8 - N_HEADS), (0, 0)))
    cum_past = _cumsum_lanes(lf_past.reshape(depth * nb * 8, past)).reshape(depth, nb, 8, past)

    x_p0, x_s0 = x_prompt.reshape(tp, d), x_sample.reshape(n_s, d)
    x, y_parts = None, None
    stacked_p, stacked_s = (), ()
    logf_p, logf_s = [], []
    for l in range(depth):
        lam_init = 0.8 - 0.6 * math.exp(-0.3 * l)
        dqg = jnp.tile(diff_q_g[l], 2)[None, :]
        dkg = jnp.tile(diff_k_g[l], 2)[None, :]
        proj_args = (norm_mix_g[l][None, :], w_in_r[l], b_f[l][None, :],
                     fox_q_g[l][None, :], fox_k_g[l][None, :], dqg, dkg, tm_in)
        sg = diff_subln_g[l][None, :]
        lam = diff_lam[l]

        res = _inproj(x_p0 if l == 0 else x, 0, tp, l, depth, stacked_p, *proj_args, True)
        sbq, sbk, sbv, fxq, fxk, fxv, dfq, dfk, dfv = res[:9]
        stacked_p = tuple(res[9:9 + N_STACKED])
        logf_p.append(res[15][:, :N_HEADS])
        cum_k = res[16][:, :N_HEADS].T.reshape(N_HEADS, tp // tk_fx, 1, tk_fx)
        o_sb = _sb_prompt(sbq, sbk, sbv, tp, tq_sb, tk_sb)
        o_fx = _fox_prompt(fxq, fxk, fxv, cum_k, tp, tq_fx, tk_fx)
        o_df = _diff_prompt(dfq, dfk, dfv, slopes, lam, sg, tp, tq_df, tk_df, lam_init)
        if l == 0:
            x = _outproj_into(x_p0, None, 0, t_all, o_sb, o_fx, o_df, w_out_b[l], tm_out)
        else:
            x = _outproj(x, 0, o_sb, o_fx, o_df, w_out_b[l], tm_out)

        if l == 0:
            res = _inproj(x_s0, 0, n_s, l, depth, stacked_s, *proj_args, False)
        else:
            res = _inproj(x, tp // tm_in, n_s, l, depth, stacked_s, *proj_args, False)
        sbq, sbk, sbv, fxq, fxk, fxv, dfq, dfk, dfv = [a.reshape(nb, ts, GROUP_W) for a in res[:9]]
        stacked_s = tuple(res[9:9 + N_STACKED])
        lf_s = res[15][:, :N_HEADS]
        logf_s.append(lf_s)
        lf_new = jnp.pad(jnp.swapaxes(lf_s.reshape(nb, ts, N_HEADS), 1, 2), ((0, 0), (0, 8 - N_HEADS), (0, 0)))
        o_sb = _sb_sample(sbq, sbk, sbv, c_sbk, c_sbv, l)
        o_fx = _fox_sample(fxq, fxk, fxv, c_fxk, c_fxv, cum_past, lf_new, l)
        o_df = _diff_sample(dfq, dfk, dfv, c_dfk, c_dfv, slopes, lam, sg, lam_init, l)
        o_s = [o.reshape(n_s, GROUP_W) for o in (o_sb, o_fx, o_df)]
        if l == 0:
            x = _outproj_into(x_s0, x, tp // tm_out, t_all, *o_s, w_out_b[l], tm_out)
        else:
            x = _outproj(x, tp // tm_out, *o_s, w_out_b[l], tm_out)

        if l % 2 == 0:
            x = _ffn_dense(x, norm_ffn_g[l][None, :], w_fg[l // 2], w_fu[l // 2], w_fd[l // 2], tm, tf)
        else:
            splits = (tp, n_s) if l == depth - 1 else (t_all,)
            y_parts = _ffn_moe(x, norm_ffn_g[l][None, :], w_mg[l // 2], w_mu[l // 2], w_md[l // 2],
                               w_router[l // 2], tm, tfe, splits)
            x = y_parts[0]
    if depth % 2 == 1:
        y_parts = [x[:tp], x[tp:]]

    kv_tail = (N_HEADS, HEAD_DIM)
    sbk_p, sbv_p, fxk_p, fxv_p, dfk_p, dfv_p = [a.reshape((depth, 1, tp) + kv_tail) for a in stacked_p]
    sbk_s, sbv_s, fxk_s, fxv_s, dfk_s, dfv_s = [a.reshape((depth, nb, ts) + kv_tail) for a in stacked_s]
    lf_p = jnp.stack(logf_p).reshape(depth, 1, tp, N_HEADS)
    lf_s = jnp.stack(logf_s).reshape(depth, nb, ts, N_HEADS)
    return (y_parts[0].reshape(1, tp, d), y_parts[1].reshape(nb, ts, d),
            sbk_p, sbv_p, fxk_p, fxv_p, lf_p, dfk_p, dfv_p,
            sbk_s, sbv_s, fxk_s, fxv_s, lf_s, dfk_s, dfv_s)
```
